```python
import jax, jax.numpy as jnp
from jax import lax
import numpy as np

D_MODEL = 1024
BATCH = 8
SEQ = 2048
DEPTH = 4
DEC_BATCH = 128
DEC_SEQ = 8
PAST_LEN = 16384
PAGE_SIZE = 128

N_BRANCH = 4
BR_WIDTH = D_MODEL // 2
SGU_GROUPS = 4
SGU_CHUNK = 128
SGU_GDIM = BR_WIDTH // SGU_GROUPS
LRU_HEADS = 4
LRU_HDIM = BR_WIDTH // LRU_HEADS
CONV_W = 4
LRU_C = 8.0
POOL_WINDOWS = (2, 4, 8, 16)
POOL_GROUPS = len(POOL_WINDOWS)
POOL_GDIM = BR_WIDTH // POOL_GROUPS
POOL_BUF = max(POOL_WINDOWS) - 1
HG_HEADS = 4
HG_DK = BR_WIDTH // HG_HEADS
HG_DV = BR_WIDTH // HG_HEADS
HG_CHUNK = 64
N_MEM = 256
X_HEADS = 4
X_HDIM = D_MODEL // X_HEADS
N_EXPERTS = 16
N_GROUPS = 4
EXPERTS_PER_GROUP = N_EXPERTS // N_GROUPS
TOP_K = 2
D_EXPERT = D_MODEL // 2
DN_ALPHA = (2 * DEPTH) ** 0.25
DN_BETA = (8 * DEPTH) ** -0.25
LN_EPS = 1e-5
RMS_EPS = 1e-6
SPLIT_POINTS = tuple(BR_WIDTH * k for k in range(1, 10))
IN_WIDTH = 9 * BR_WIDTH + N_BRANCH * D_MODEL

kernel_name = 'hybrid_gated_sgu_rglru_pool_hgrn2_moe_step'


def layer_norm(x, g, b):
    xf = x.astype(jnp.float32)
    mu = jnp.mean(xf, axis=-1, keepdims=True)
    var = jnp.mean(jnp.square(xf - mu), axis=-1, keepdims=True)
    return ((xf - mu) * lax.rsqrt(var + LN_EPS) * g + b).astype(x.dtype)


def chunk_sgu(u, v, ln_g, ln_b, w_s, b_s):
    B, T, _ = u.shape
    v = layer_norm(v, ln_g, ln_b)
    L = SGU_CHUNK if T % SGU_CHUNK == 0 else T
    n = T // L
    mask = jnp.tril(jnp.ones((L, L), dtype=bool))
    w = jnp.where(mask, w_s[:, :L, :L], 0.0)
    vg = v.reshape(B, n, L, SGU_GROUPS, SGU_GDIM)
    mixed = jnp.einsum('gts,bnsgc->bntgc', w, vg) + b_s[:, :L].T[None, None, :, :, None]
    return u * mixed.reshape(B, T, BR_WIDTH).astype(u.dtype), v


def causal_conv(x, buf, w, b):
    T = x.shape[1]
    xp = jnp.concatenate([buf.astype(x.dtype), x], axis=1)
    y = b + xp[:, 0:T] * w[0]
    for j in range(1, CONV_W):
        y = y + xp[:, j:j + T] * w[j]
    return y, xp[:, -(CONV_W - 1):]


def rg_lru(x, h0, w_r, b_r, w_i, b_i, lam):
    B, T, _ = x.shape
    xf = x.astype(jnp.float32)
    xh = xf.reshape(B, T, LRU_HEADS, LRU_HDIM)
    r = jax.nn.sigmoid(jnp.einsum('bthc,hcd->bthd', xh, w_r).reshape(B, T, BR_WIDTH) + b_r)
    i = jax.nn.sigmoid(jnp.einsum('bthc,hcd->bthd', xh, w_i).reshape(B, T, BR_WIDTH) + b_i)
    log_a = -LRU_C * r * jax.nn.softplus(-lam.astype(jnp.float32))
    a = jnp.exp(log_a)
    bx = jnp.sqrt(-jnp.expm1(2.0 * log_a)) * (i * xf)
    bx = bx.at[:, 0].add(a[:, 0] * h0.astype(jnp.float32))

    def combine(lhs, rhs):
        return (lhs[0] * rhs[0], rhs[0] * lhs[1] + rhs[1])

    _, h = lax.associative_scan(combine, (a, bx), axis=1)
    return h.astype(x.dtype), h[:, -1].astype(h0.dtype)


def pool_mixer(c, buf, pos0, w_p, b_p, scale):
    B, T, _ = c.shape
    cat = jnp.concatenate([buf.astype(c.dtype), c], axis=1)
    catf = cat.astype(jnp.float32)
    cs = jnp.concatenate([jnp.zeros_like(catf[:, :1]), jnp.cumsum(catf, axis=1)], axis=1)
    end = cs[:, POOL_BUF + 1:]
    pooled = []
    for g, w in enumerate(POOL_WINDOWS):
        sl = slice(g * POOL_GDIM, (g + 1) * POOL_GDIM)
        s = end[..., sl] - cs[:, POOL_BUF + 1 - w:POOL_BUF + 1 - w + T, sl]
        cnt = jnp.minimum(w, pos0 + 1 + jnp.arange(T)).astype(jnp.float32)
        pooled.append(s / cnt[None, :, None])
    d = (jnp.concatenate(pooled, axis=-1) - c.astype(jnp.float32)).reshape(B, T, POOL_GROUPS, POOL_GDIM)
    y = (jnp.einsum('btgc,gcd->btgd', d, w_p).reshape(B, T, BR_WIDTH) + b_p) * scale
    return y.astype(c.dtype), cat[:, -POOL_BUF:]


def hgrn2(q, f, i, g_out, s0, lb, norm_g):
    B, T, _ = q.shape
    f32 = jnp.float32
    ff = f.astype(f32)
    log_f = jnp.logaddexp(jnp.log(lb), jnp.log1p(-lb) + jax.nn.log_sigmoid(ff))
    k = (1.0 - lb) * jax.nn.sigmoid(-ff)
    qf = jax.nn.silu(q.astype(f32))
    vf = i.astype(f32)
    L = HG_CHUNK if T % HG_CHUNK == 0 else T
    n = T // L

    def to_chunks(a, d):
        return a.reshape(B, n, L, HG_HEADS, d).swapaxes(0, 1)

    mask = jnp.tril(jnp.ones((L, L), dtype=bool))[None, :, :, None, None]

    def step(S, inp):
        qc, kc, vc, lfc = inp
        bcum = jnp.cumsum(lfc, axis=1)
        o_inter = jnp.einsum('bthk,bhkv->bthv', qc * jnp.exp(bcum), S)
        diff = bcum[:, :, None] - bcum[:, None, :]
        decay = jnp.exp(jnp.where(mask, diff, -jnp.inf))
        att = jnp.einsum('bthk,bshk,btshk->bhts', qc, kc, decay)
        o_intra = jnp.einsum('bhts,bshv->bthv', att, vc)
        b_last = bcum[:, -1]
        kd = kc * jnp.exp(b_last[:, None] - bcum)
        S_new = jnp.exp(b_last)[..., None] * S + jnp.einsum('bshk,bshv->bhkv', kd, vc)
        return S_new, o_inter + o_intra

    s_t, o = lax.scan(step, s0.astype(f32),
                      (to_chunks(qf, HG_DK), to_chunks(k, HG_DK), to_chunks(vf, HG_DV), to_chunks(log_f, HG_DK)))
    o = o.swapaxes(0, 1).reshape(B, T, HG_HEADS, HG_DV)
    o = o * lax.rsqrt(jnp.mean(o * o, axis=-1, keepdims=True) + RMS_EPS) * norm_g
    o = o.reshape(B, T, BR_WIDTH) * jax.nn.silu(g_out.astype(f32))
    return o.astype(q.dtype), s_t.astype(s0.dtype)


def memory_kv(mem, wk, wv):
    B, M, _ = mem.shape
    k = jnp.einsum('bmd,de->bme', mem, wk).reshape(B, M, X_HEADS, X_HDIM)
    v = jnp.einsum('bmd,de->bme', mem, wv).reshape(B, M, X_HEADS, X_HDIM)
    return k, v


def cross_attend(x, mem_k, mem_v, wq, wo):
    B, T, _ = x.shape
    q = jnp.einsum('btd,de->bte', x, wq).reshape(B, T, X_HEADS, X_HDIM)
    s = jnp.einsum('bthd,bmhd->bhtm', q.astype(jnp.float32), mem_k.astype(jnp.float32)) * (X_HDIM ** -0.5)
    p = jax.nn.softmax(s, axis=-1)
    o = jnp.einsum('bhtm,bmhd->bthd', p, mem_v.astype(jnp.float32)).reshape(B, T, D_MODEL).astype(x.dtype)
    return jnp.einsum('btd,de->bte', o, wo)


def moe(x, router_w, router_b, e_wg, e_wu, e_wd):
    B, T, D = x.shape
    xt = x.reshape(B * T, D)
    f32 = jnp.float32
    s = jax.nn.sigmoid(jnp.einsum('nd,de->ne', xt.astype(f32), router_w.astype(f32)))
    sel = (s + router_b.astype(f32)).reshape(-1, N_GROUPS, EXPERTS_PER_GROUP)
    group_score = lax.top_k(sel, TOP_K)[0].sum(axis=-1)
    g_idx = jnp.argmax(group_score, axis=-1)
    g_take = jnp.broadcast_to(g_idx[:, None, None], (g_idx.shape[0], 1, EXPERTS_PER_GROUP))
    in_group = jnp.take_along_axis(sel, g_take, axis=1)[:, 0]
    _, loc = lax.top_k(in_group, TOP_K)
    e_idx = g_idx[:, None] * EXPERTS_PER_GROUP + loc
    w = jnp.take_along_axis(s, e_idx, axis=-1)
    w = w / jnp.sum(w, axis=-1, keepdims=True)
    combine = jnp.einsum('nk,nke->ne', w, jax.nn.one_hot(e_idx, N_EXPERTS, dtype=f32))
    y = jnp.zeros((B * T, D), f32)
    for e in range(N_EXPERTS):
        h = jax.nn.silu(xt @ e_wg[e]) * (xt @ e_wu[e])
        y = y + combine[:, e:e + 1] * (h @ e_wd[e])
    return y.reshape(B, T, D).astype(x.dtype)


def mixer_block(x, conv_buf, h0, pool_buf, s0, pos0, lb, p):
    B, T, _ = x.shape
    z = jnp.einsum('btd,de->bte', x, p['w_in']) + p['b_in']
    u, v, yg, xb, c, q, f, i, go, gz = jnp.split(z, SPLIT_POINTS, axis=-1)
    o_a, v_rows = chunk_sgu(u, v, p['sgu_ln_g'], p['sgu_ln_b'], p['sgu_w'], p['sgu_b'])
    xc, conv_new = causal_conv(xb, conv_buf, p['conv_w'], p['conv_b'])
    h, h_new = rg_lru(xc, h0, p['lru_wr'], p['lru_br'], p['lru_wi'], p['lru_bi'], p['lru_lambda'])
    o_b = jax.nn.gelu(yg) * h
    o_c, pool_new = pool_mixer(c, pool_buf, pos0, p['pool_w'], p['pool_b'], p['pool_scale'])
    o_d, s_new = hgrn2(q, f, i, go, s0, lb, p['hgrn_norm_g'])
    branches = jnp.stack([o_a, o_b, o_c, o_d], axis=2)
    proj = jnp.einsum('btgc,gcd->btgd', branches, p['w_branch'])
    gates = jax.nn.sigmoid(gz.reshape(B, T, N_BRANCH, D_MODEL))
    y = jnp.einsum('btd,de->bte', jnp.sum(gates * proj, axis=2), p['w_out'])
    return y, (conv_new, h_new, pool_new, s_new, v_rows)


def trunk_layer(x, conv_buf, h0, pool_buf, s0, mem_k, mem_v, pos0, lb, p):
    mix, states = mixer_block(x, conv_buf, h0, pool_buf, s0, pos0, lb, p)
    x = layer_norm(DN_ALPHA * x + mix, p['ln_g'][0], p['ln_b'][0])
    x = layer_norm(DN_ALPHA * x + cross_attend(x, mem_k, mem_v, p['x_wq'], p['x_wo']), p['ln_g'][1], p['ln_b'][1])
    x = layer_norm(DN_ALPHA * x + moe(x, p['router_w'], p['router_b'], p['e_wg'], p['e_wu'], p['e_wd']),
                   p['ln_g'][2], p['ln_b'][2])
    return x, states


def setup_inputs(seed: int = 0) -> dict:
    key = jax.random.key(seed)
    keys = iter(jax.random.split(key, 64))
    f32 = jnp.float32

    def nrm(shape, scale):
        return jax.random.normal(next(keys), shape, f32) * scale

    d_s = D_MODEL ** -0.5
    u = jax.random.uniform(next(keys), (DEPTH, BR_WIDTH), f32, 0.9, 0.999)
    a = u ** (1.0 / LRU_C)
    inp = {}
    inp['x_prompt'] = nrm((BATCH, SEQ, D_MODEL), 1.0)
    inp['x_sample'] = nrm((DEC_BATCH, DEC_SEQ, D_MODEL), 1.0)
    inp['mem_prompt'] = nrm((BATCH, N_MEM, D_MODEL), 1.0)
    inp['cache_mem_k'] = nrm((DEPTH, DEC_BATCH, N_MEM, X_HEADS, X_HDIM), 1.0)
    inp['cache_mem_v'] = nrm((DEPTH, DEC_BATCH, N_MEM, X_HEADS, X_HDIM), DN_BETA)
    inp['state_conv'] = nrm((DEPTH, DEC_BATCH, CONV_W - 1, BR_WIDTH), 1.0)
    inp['state_rglru'] = nrm((DEPTH, DEC_BATCH, BR_WIDTH), 0.5)
    inp['state_pool'] = nrm((DEPTH, DEC_BATCH, POOL_BUF, BR_WIDTH), 1.0)
    inp['state_hgrn'] = nrm((DEPTH, DEC_BATCH, HG_HEADS, HG_DK, HG_DV), 1.0)
    inp['ln_in_g'] = 1.0 + nrm((D_MODEL,), 0.1)
    inp['ln_in_b'] = nrm((D_MODEL,), 0.02)
    inp['w_in'] = nrm((DEPTH, D_MODEL, IN_WIDTH), d_s)
    inp['b_in'] = nrm((DEPTH, IN_WIDTH), 0.02)
    inp['sgu_ln_g'] = 1.0 + nrm((DEPTH, BR_WIDTH), 0.1)
    inp['sgu_ln_b'] = nrm((DEPTH, BR_WIDTH), 0.02)
    inp['sgu_w'] = nrm((DEPTH, SGU_GROUPS, SGU_CHUNK, SGU_CHUNK), SGU_CHUNK ** -0.5)
    inp['sgu_b'] = 1.0 + nrm((DEPTH, SGU_GROUPS, SGU_CHUNK), 0.1)
    inp['conv_w'] = nrm((DEPTH, CONV_W, BR_WIDTH), CONV_W ** -0.5)
    inp['conv_b'] = nrm((DEPTH, BR_WIDTH), 0.02)
    inp['lru_wr'] = nrm((DEPTH, LRU_HEADS, LRU_HDIM, LRU_HDIM), LRU_HDIM ** -0.5)
    inp['lru_br'] = nrm((DEPTH, BR_WIDTH), 0.02)
    inp['lru_wi'] = nrm((DEPTH, LRU_HEADS, LRU_HDIM, LRU_HDIM), LRU_HDIM ** -0.5)
    inp['lru_bi'] = nrm((DEPTH, BR_WIDTH), 0.02)
    inp['lru_lambda'] = jnp.log(a) - jnp.log1p(-a)
    inp['pool_w'] = nrm((DEPTH, POOL_GROUPS, POOL_GDIM, POOL_GDIM), POOL_GDIM ** -0.5)
    inp['pool_b'] = nrm((DEPTH, BR_WIDTH), 0.02)
    inp['pool_scale'] = 1.0 + nrm((DEPTH, BR_WIDTH), 0.1)
    inp['hgrn_lb'] = nrm((DEPTH, BR_WIDTH), 1.0)
    inp['hgrn_norm_g'] = 1.0 + nrm((DEPTH, HG_DV), 0.1)
    inp['w_branch'] = nrm((DEPTH, N_BRANCH, BR_WIDTH, D_MODEL), BR_WIDTH ** -0.5)
    inp['w_out'] = nrm((DEPTH, D_MODEL, D_MODEL), d_s * DN_BETA)
    inp['x_wq'] = nrm((DEPTH, D_MODEL, D_MODEL), d_s)
    inp['x_wk'] = nrm((DEPTH, D_MODEL, D_MODEL), d_s)
    inp['x_wv'] = nrm((DEPTH, D_MODEL, D_MODEL), d_s * DN_BETA)
    inp['x_wo'] = nrm((DEPTH, D_MODEL, D_MODEL), d_s * DN_BETA)
    inp['ln_g'] = 1.0 + nrm((DEPTH, 3, D_MODEL), 0.1)
    inp['ln_b'] = nrm((DEPTH, 3, D_MODEL), 0.02)
    inp['router_w'] = nrm((D_MODEL, N_EXPERTS), d_s)
    inp['router_b'] = nrm((N_EXPERTS,), 0.01)
    inp['e_wg'] = nrm((DEPTH, N_EXPERTS, D_MODEL, D_EXPERT), d_s)
    inp['e_wu'] = nrm((DEPTH, N_EXPERTS, D_MODEL, D_EXPERT), d_s)
    inp['e_wd'] = nrm((DEPTH, N_EXPERTS, D_EXPERT, D_MODEL), D_EXPERT ** -0.5 * DN_BETA)
    return inp


def reference(x_prompt, x_sample, mem_prompt, cache_mem_k, cache_mem_v, state_conv, state_rglru, state_pool,
              state_hgrn, ln_in_g, ln_in_b, w_in, b_in, sgu_ln_g, sgu_ln_b, sgu_w, sgu_b, conv_w, conv_b,
              lru_wr, lru_br, lru_wi, lru_bi, lru_lambda, pool_w, pool_b, pool_scale, hgrn_lb, hgrn_norm_g,
              w_branch, w_out, x_wq, x_wk, x_wv, x_wo, ln_g, ln_b, router_w, router_b, e_wg, e_wu, e_wd):
    f32 = jnp.float32
    lb_cum = jnp.cumsum(jax.nn.softmax(hgrn_lb.astype(f32), axis=0), axis=0)
    lbs = lb_cum - lb_cum[0]
    xp = layer_norm(x_prompt, ln_in_g, ln_in_b)
    xs = layer_norm(x_sample, ln_in_g, ln_in_b)
    dt = x_prompt.dtype
    zero_conv = jnp.zeros((BATCH, CONV_W - 1, BR_WIDTH), dt)
    zero_h = jnp.zeros((BATCH, BR_WIDTH), dt)
    zero_pool = jnp.zeros((BATCH, POOL_BUF, BR_WIDTH), dt)
    zero_s = jnp.zeros((BATCH, HG_HEADS, HG_DK, HG_DV), dt)
    mk_pl, mv_pl, cp_l, hp_l, pp_l, sp_l = [], [], [], [], [], []
    cs_l, hs_l, ps_l, ss_l, vs_l = [], [], [], [], []
    for l in range(DEPTH):
        p = dict(w_in=w_in[l], b_in=b_in[l], sgu_ln_g=sgu_ln_g[l], sgu_ln_b=sgu_ln_b[l], sgu_w=sgu_w[l],
                 sgu_b=sgu_b[l], conv_w=conv_w[l], conv_b=conv_b[l], lru_wr=lru_wr[l], lru_br=lru_br[l],
                 lru_wi=lru_wi[l], lru_bi=lru_bi[l], lru_lambda=lru_lambda[l], pool_w=pool_w[l],
                 pool_b=pool_b[l], pool_scale=pool_scale[l], hgrn_norm_g=hgrn_norm_g[l],
                 w_branch=w_branch[l], w_out=w_out[l], x_wq=x_wq[l], x_wo=x_wo[l], ln_g=ln_g[l], ln_b=ln_b[l],
                 router_w=router_w, router_b=router_b, e_wg=e_wg[l], e_wu=e_wu[l], e_wd=e_wd[l])
        mk_p, mv_p = memory_kv(mem_prompt, x_wk[l], x_wv[l])
        xp, (cp, hp, pp, sp, _) = trunk_layer(xp, zero_conv, zero_h, zero_pool, zero_s, mk_p, mv_p, 0, lbs[l], p)
        xs, (cs_, hs, ps, ss, vs) = trunk_layer(xs, state_conv[l], state_rglru[l], state_pool[l], state_hgrn[l],
                                                cache_mem_k[l], cache_mem_v[l], PAST_LEN, lbs[l], p)
        mk_pl.append(mk_p); mv_pl.append(mv_p); cp_l.append(cp); hp_l.append(hp); pp_l.append(pp); sp_l.append(sp)
        cs_l.append(cs_); hs_l.append(hs); ps_l.append(ps); ss_l.append(ss); vs_l.append(vs)
    return (xp, xs, jnp.stack(mk_pl), jnp.stack(mv_pl), jnp.stack(cp_l), jnp.stack(hp_l), jnp.stack(pp_l),
            jnp.stack(sp_l), jnp.stack(cs_l), jnp.stack(hs_l), jnp.stack(ps_l), jnp.stack(ss_l), jnp.stack(vs_l))
```

```python
import functools

import jax
import jax.numpy as jnp
from jax import lax
from jax.experimental import pallas as pl
from jax.experimental.pallas import tpu as pltpu

F32 = jnp.float32
BF16 = jnp.bfloat16

N_BRANCH = 4
SGU_GROUPS = 4
SGU_CHUNK = 128
LRU_HEADS = 4
CONV_W = 4
LRU_C = 8.0
POOL_WINDOWS = (2, 4, 8, 16)
POOL_BUF = max(POOL_WINDOWS) - 1
HG_HEADS = 4
X_HEADS = 4
N_EXPERTS = 16
N_GROUPS = 4
EXPERTS_PER_GROUP = N_EXPERTS // N_GROUPS
MODEL_DEPTH = 4
DN_ALPHA = (2 * MODEL_DEPTH) ** 0.25
LN_EPS = 1e-5
RMS_EPS = 1e-6
PAST_LEN = 16384

V7X_VMEM_BYTES = 64 * 1024 * 1024
VMEM_LIMIT = V7X_VMEM_BYTES - 8 * 1024 * 1024
SUBLANES = 8
LANES = 128


def _params(n_grid):
    return pltpu.CompilerParams(dimension_semantics=("arbitrary",) * n_grid, vmem_limit_bytes=VMEM_LIMIT)


def _resident(shape):
    nd = len(shape)
    return pl.BlockSpec(shape, lambda *_: (0,) * nd, pipeline_mode=pl.Buffered(1))


def _layer_norm(x, g, b):
    mu = jnp.mean(x, axis=-1, keepdims=True)
    xc = x - mu
    var = jnp.mean(xc * xc, axis=-1, keepdims=True)
    return xc * lax.rsqrt(var + LN_EPS) * g + b


def _sigmoid(x):
    return 1.0 / (1.0 + jnp.exp(-x))


def _bdot(a, b):
    return jnp.dot(a.astype(BF16), b.astype(BF16), preferred_element_type=F32)


def _bdot_nt(a, b):
    return lax.dot_general(a.astype(BF16), b.astype(BF16), (((1,), (1,)), ((), ())), preferred_element_type=F32)


def _bdot_tn(a, b):
    return lax.dot_general(a.astype(BF16), b.astype(BF16), (((0,), (0,)), ((), ())), preferred_element_type=F32)


def _shift_rows(x, k):
    return pltpu.roll(x, k, 0)


def _ln_kernel(x_ref, g_ref, b_ref, o_ref):
    o_ref[...] = _layer_norm(x_ref[...], g_ref[...], b_ref[...])


def _input_ln(x, g, b, tile):
    n, d = x.shape
    return pl.pallas_call(
        _ln_kernel,
        grid=(n // tile,),
        in_specs=[pl.BlockSpec((tile, d), lambda i: (i, 0)), _resident((1, d)), _resident((1, d))],
        out_specs=pl.BlockSpec((tile, d), lambda i: (i, 0)),
        out_shape=jax.ShapeDtypeStruct((n, d), F32),
        compiler_params=_params(1),
        name="input_ln",
    )(x, g.reshape(1, d), b.reshape(1, d))


def _kv_kernel(m_ref, wk_ref, wv_ref, k_ref, v_ref):
    m = m_ref[...].astype(BF16)
    k_ref[...] = jnp.dot(m, wk_ref[...], preferred_element_type=F32)
    v_ref[...] = jnp.dot(m, wv_ref[...], preferred_element_type=F32)


def _memory_kv(mem, wk, wv, tile):
    n, d = mem.shape
    depth = wk.shape[0]
    out = jax.ShapeDtypeStruct((depth, n, d), F32)
    return pl.pallas_call(
        _kv_kernel,
        grid=(depth, n // tile),
        in_specs=[
            pl.BlockSpec((tile, d), lambda l, i: (i, 0)),
            pl.BlockSpec((None, d, d), lambda l, i: (l, 0, 0)),
            pl.BlockSpec((None, d, d), lambda l, i: (l, 0, 0)),
        ],
        out_specs=[pl.BlockSpec((None, tile, d), lambda l, i: (l, i, 0))] * 2,
        out_shape=[out, out],
        compiler_params=_params(2),
        name="memory_kv",
    )(mem, wk, wv)


def _seqmix_kernel(*refs, tile, sub, pos0, has_state, emit_v, br):
    (x_ref, win_ref, bin_ref, sgug_ref, sgub_ref, sguw_ref, sgubt_ref, convw_ref, convb_ref, wr_ref, br_ref,
     wi_ref, bi_ref, lam_ref, poolw_ref, poolb_ref, pools_ref, lb_ref, hgn_ref) = refs[:19]
    refs = refs[19:]
    if has_state:
        conv0_ref, h0_ref, pool0_ref, s0_ref = refs[:4]
        refs = refs[4:]
    obr_ref, convn_ref, hn_ref, pooln_ref, sn_ref = refs[:5]
    refs = refs[5:]
    if emit_v:
        vrows_ref = refs[0]
        refs = refs[1:]
    conv_sc, pool_sc, h_sc, st_sc = refs

    t = pl.program_id(1)
    n_t = pl.num_programs(1)
    gd = br // SGU_GROUPS
    hd = br // HG_HEADS

    @pl.when(t == 0)
    def _init():
        conv_sc[...] = jnp.zeros_like(conv_sc)
        pool_sc[...] = jnp.zeros_like(pool_sc)
        if has_state:
            conv_sc[SUBLANES - (CONV_W - 1):, :] = conv0_ref[...]
            pool_sc[1:, :] = pool0_ref[...]
            h_sc[...] = h0_ref[...]
            for h in range(HG_HEADS):
                st_sc[h] = s0_ref[h].T
        else:
            h_sc[...] = jnp.zeros_like(h_sc)
            st_sc[...] = jnp.zeros_like(st_sc)

    x = x_ref[...]
    z = jnp.dot(x.astype(BF16), win_ref[...], preferred_element_type=F32) + bin_ref[...]

    def col(k):
        return z[:, k * br:(k + 1) * br]

    u, v, yg, xb, c, q, f, vi, go = [col(k) for k in range(9)]
    row = lax.broadcasted_iota(jnp.int32, (tile, br), 0)

    vn = _layer_norm(v, sgug_ref[...], sgub_ref[...])
    if emit_v:
        vrows_ref[...] = vn
    rr = lax.broadcasted_iota(jnp.int32, (tile, tile), 0)
    cc = lax.broadcasted_iota(jnp.int32, (tile, tile), 1)
    o_a = []
    for g in range(SGU_GROUPS):
        w = jnp.where(rr >= cc, sguw_ref[g, :tile, :tile], 0.0)
        mixed = _bdot(w, vn[:, g * gd:(g + 1) * gd]) + sgubt_ref[:tile, g:g + 1]
        o_a.append(u[:, g * gd:(g + 1) * gd] * mixed)
    o_a = jnp.concatenate(o_a, axis=1)

    ext = jnp.concatenate([conv_sc[...], xb], axis=0)
    cw = convw_ref[...]
    base = SUBLANES - (CONV_W - 1)
    xc = convb_ref[...] + xb * cw[CONV_W - 1:CONV_W]
    for j in range(CONV_W - 1):
        xc = xc + ext[base + j:base + j + tile] * cw[j:j + 1]
    conv_sc[...] = ext[tile:tile + SUBLANES]
    convn_ref[...] = ext[tile + base:tile + SUBLANES]

    def heads_dot(a, w_ref):
        return jnp.concatenate([_bdot(a[:, h * hd:(h + 1) * hd], w_ref[h]) for h in range(LRU_HEADS)], axis=1)

    r_g = _sigmoid(heads_dot(xc, wr_ref) + br_ref[...])
    i_g = _sigmoid(heads_dot(xc, wi_ref) + bi_ref[...])
    lam = lam_ref[...]
    softplus_neg_lam = jnp.maximum(-lam, 0.0) + jnp.log1p(jnp.exp(-jnp.abs(lam)))
    log_a = (-LRU_C) * r_g * softplus_neg_lam
    a = jnp.exp(log_a)
    y2 = 2.0 * log_a
    one_minus = jnp.tanh(-0.5 * y2) * (jnp.exp(y2) + 1.0)
    bx = jnp.sqrt(one_minus) * (i_g * xc)
    sa, sb = a, bx
    k = 1
    while k < tile:
        keep = row >= k
        sb = sa * jnp.where(keep, _shift_rows(sb, k), 0.0) + sb
        sa = sa * jnp.where(keep, _shift_rows(sa, k), 1.0)
        k *= 2
    hseq = sb + sa * h_sc[...]
    h_sc[...] = hseq[tile - 1:tile]
    hn_ref[...] = hseq[tile - 1:tile]
    gelu = 0.5 * yg * (1.0 + jnp.tanh(0.7978845608028654 * (yg + 0.044715 * (yg * yg * yg))))
    o_b = gelu * hseq

    extp = jnp.concatenate([pool_sc[...], c], axis=0)
    pb = POOL_BUF + 1
    sums = {1: extp}
    wdt = 1
    while wdt < max(POOL_WINDOWS):
        sums[2 * wdt] = sums[wdt] + _shift_rows(sums[wdt], wdt)
        wdt *= 2
    tpos = pos0 + 1 + t * tile + lax.broadcasted_iota(jnp.int32, (tile, 1), 0)
    o_c = []
    for g, wdw in enumerate(POOL_WINDOWS):
        cnt = jnp.minimum(wdw, tpos).astype(F32)
        pooled = sums[wdw][pb:pb + tile, g * gd:(g + 1) * gd] / cnt
        o_c.append(_bdot(pooled - c[:, g * gd:(g + 1) * gd], poolw_ref[g]))
    o_c = (jnp.concatenate(o_c, axis=1) + poolb_ref[...]) * pools_ref[...]
    pool_sc[...] = extp[tile:tile + pb]
    pooln_ref[...] = extp[tile + 1:tile + pb]

    lb = lb_ref[...]
    log_sig = jnp.minimum(f, 0.0) - jnp.log1p(jnp.exp(-jnp.abs(f)))
    a1 = jnp.log(lb)
    a2 = jnp.log1p(-lb) + log_sig
    log_f = jnp.maximum(a1, a2) + jnp.log1p(jnp.exp(-jnp.abs(a1 - a2)))
    kf = (1.0 - lb) * _sigmoid(-f)
    qf = q * _sigmoid(q)
    rmod = row % sub
    bc = log_f
    k = 1
    while k < sub:
        bc = bc + jnp.where(rmod >= k, _shift_rows(bc, k), 0.0)
        k *= 2
    o_d = [jnp.zeros((tile, hd), F32) for _ in range(HG_HEADS)]
    for d in range(sub):
        if d == 0:
            p = qf * kf
            vd = vi
        else:
            diff = jnp.where(rmod >= d, bc - _shift_rows(bc, d), -jnp.inf)
            p = qf * _shift_rows(kf, d) * jnp.exp(diff)
            vd = _shift_rows(vi, d)
        for h in range(HG_HEADS):
            att = jnp.sum(p[:, h * hd:(h + 1) * hd], axis=-1, keepdims=True)
            o_d[h] = o_d[h] + att * vd[:, h * hd:(h + 1) * hd]
    qe = qf * jnp.exp(bc)
    for j in range(tile // sub):
        lo, hi = j * sub, (j + 1) * sub
        b_last = bc[hi - 1:hi]
        kdec = kf[lo:hi] * jnp.exp(b_last - bc[lo:hi])
        decay = jnp.exp(b_last)
        inter = []
        for h in range(HG_HEADS):
            st = st_sc[h]
            inter.append(_bdot_nt(qe[lo:hi, h * hd:(h + 1) * hd], st))
            upd = _bdot_tn(vi[lo:hi, h * hd:(h + 1) * hd], kdec[:, h * hd:(h + 1) * hd])
            st_sc[h] = st * decay[:, h * hd:(h + 1) * hd] + upd
        inter = jnp.concatenate(inter, axis=1)
        if j == 0:
            o_inter = [inter]
        else:
            o_inter.append(inter)
    o_inter = jnp.concatenate(o_inter, axis=0) if len(o_inter) > 1 else o_inter[0]
    gn = hgn_ref[...]
    o_n = []
    for h in range(HG_HEADS):
        oh = o_d[h] + o_inter[:, h * hd:(h + 1) * hd]
        ms = jnp.mean(oh * oh, axis=-1, keepdims=True)
        o_n.append(oh * lax.rsqrt(ms + RMS_EPS) * gn)
    o_d = jnp.concatenate(o_n, axis=1) * (go * _sigmoid(go))

    obr_ref[...] = jnp.concatenate([o_a, o_b, o_c, o_d], axis=1).astype(BF16)

    @pl.when(t == n_t - 1)
    def _emit_state():
        for h in range(HG_HEADS):
            sn_ref[h] = st_sc[h].T


def _seqmix(x, p, lb, states, *, batch, seq, tile, sub, pos0, emit_v):
    n, d = x.shape
    br = d // 2
    hd = br // HG_HEADS
    n_t = seq // tile
    has_state = states is not None
    mixw = 9 * br
    row = lambda a: a.reshape(1, -1)
    ins = [x, p["w_in_mix"], row(p["b_in"][:mixw]), row(p["sgu_ln_g"]), row(p["sgu_ln_b"]), p["sgu_w"], p["sgu_b"].T,
           p["conv_w"], row(p["conv_b"]), p["lru_wr"], row(p["lru_br"]), p["lru_wi"], row(p["lru_bi"]),
           row(p["lru_lambda"]), p["pool_w"], row(p["pool_b"]), row(p["pool_scale"]), row(lb), row(p["hgrn_norm_g"])]
    in_specs = [pl.BlockSpec((tile, d), lambda b, t: (b * n_t + t, 0))] + [_resident(a.shape) for a in ins[1:]]
    if has_state:
        conv0, h0, pool0, s0 = states
        ins += [conv0, h0.reshape(batch, 1, br), pool0, s0]
        in_specs += [
            pl.BlockSpec((None, CONV_W - 1, br), lambda b, t: (b, 0, 0)),
            pl.BlockSpec((None, 1, br), lambda b, t: (b, 0, 0)),
            pl.BlockSpec((None, POOL_BUF, br), lambda b, t: (b, 0, 0)),
            pl.BlockSpec((None, HG_HEADS, hd, hd), lambda b, t: (b, 0, 0, 0)),
        ]
    out_shape = [
        jax.ShapeDtypeStruct((n, N_BRANCH * br), BF16),
        jax.ShapeDtypeStruct((batch, CONV_W - 1, br), F32),
        jax.ShapeDtypeStruct((batch, 1, br), F32),
        jax.ShapeDtypeStruct((batch, POOL_BUF, br), F32),
        jax.ShapeDtypeStruct((batch, HG_HEADS, hd, hd), F32),
    ]
    out_specs = [
        pl.BlockSpec((tile, N_BRANCH * br), lambda b, t: (b * n_t + t, 0)),
        pl.BlockSpec((None, CONV_W - 1, br), lambda b, t: (b, 0, 0)),
        pl.BlockSpec((None, 1, br), lambda b, t: (b, 0, 0)),
        pl.BlockSpec((None, POOL_BUF, br), lambda b, t: (b, 0, 0)),
        pl.BlockSpec((None, HG_HEADS, hd, hd), lambda b, t: (b, 0, 0, 0)),
    ]
    if emit_v:
        out_shape.append(jax.ShapeDtypeStruct((n, br), F32))
        out_specs.append(pl.BlockSpec((tile, br), lambda b, t: (b * n_t + t, 0)))
    scratch = [
        pltpu.VMEM((SUBLANES, br), F32),
        pltpu.VMEM((POOL_BUF + 1, br), F32),
        pltpu.VMEM((1, br), F32),
        pltpu.VMEM((HG_HEADS, hd, hd), F32),
    ]
    kern = functools.partial(_seqmix_kernel, tile=tile, sub=sub, pos0=pos0, has_state=has_state, emit_v=emit_v, br=br)
    outs = pl.pallas_call(
        kern,
        grid=(batch, n_t),
        in_specs=in_specs,
        out_specs=out_specs,
        out_shape=out_shape,
        scratch_shapes=scratch,
        compiler_params=_params(2),
        name="seqmix_state" if has_state else "seqmix",
    )(*ins)
    return outs


def _merge_kernel(x_ref, obr_ref, wg_ref, bg_ref, wb_ref, wo_ref, lng_ref, lnb_ref, wq_ref, x1_ref, q_ref, *, br, d):
    x = x_ref[...]
    xb16 = x.astype(BF16)
    acc = None
    for g in range(N_BRANCH):
        gz = jnp.dot(xb16, wg_ref[:, g * d:(g + 1) * d], preferred_element_type=F32) + bg_ref[:, g * d:(g + 1) * d]
        proj = jnp.dot(obr_ref[:, g * br:(g + 1) * br], wb_ref[g], preferred_element_type=F32)
        term = _sigmoid(gz) * proj
        acc = term if acc is None else acc + term
    y = _bdot(acc, wo_ref[...])
    x1 = _layer_norm(DN_ALPHA * x + y, lng_ref[...], lnb_ref[...])
    x1_ref[...] = x1
    q_ref[...] = _bdot(x1, wq_ref[...]).astype(BF16)


def _merge(x, obr, p, tile):
    n, d = x.shape
    br = d // 2
    ins = [x, obr, p["w_in_gate"], p["b_in"][9 * br:].reshape(1, -1), p["w_branch"], p["w_out"],
           p["ln_g"][0].reshape(1, d), p["ln_b"][0].reshape(1, d), p["x_wq"]]
    in_specs = [pl.BlockSpec((tile, d), lambda i: (i, 0)), pl.BlockSpec((tile, N_BRANCH * br), lambda i: (i, 0))]
    in_specs += [_resident(a.shape) for a in ins[2:]]
    return pl.pallas_call(
        functools.partial(_merge_kernel, br=br, d=d),
        grid=(n // tile,),
        in_specs=in_specs,
        out_specs=[pl.BlockSpec((tile, d), lambda i: (i, 0))] * 2,
        out_shape=[jax.ShapeDtypeStruct((n, d), F32), jax.ShapeDtypeStruct((n, d), BF16)],
        compiler_params=_params(1),
        name="merge",
    )(*ins)


def _attn_kernel(q_ref, k_ref, v_ref, o_ref, *, hd):
    scale = hd ** -0.5
    outs = []
    for h in range(X_HEADS):
        sl = slice(h * hd, (h + 1) * hd)
        s = _bdot_nt(q_ref[:, sl], k_ref[:, sl]) * scale
        e = jnp.exp(s - jnp.max(s, axis=-1, keepdims=True))
        prob = e / jnp.sum(e, axis=-1, keepdims=True)
        outs.append(_bdot(prob, v_ref[:, sl]))
    o_ref[...] = jnp.concatenate(outs, axis=1).astype(BF16)


def _attn(q, mem_k, mem_v, layer, *, batch, seq, tile):
    n, d = q.shape
    n_t = seq // tile
    m = mem_k.shape[2]
    kv_spec = pl.BlockSpec((None, None, m, d), lambda b, t: (layer, b, 0, 0))
    return pl.pallas_call(
        functools.partial(_attn_kernel, hd=d // X_HEADS),
        grid=(batch, n_t),
        in_specs=[pl.BlockSpec((tile, d), lambda b, t: (b * n_t + t, 0)), kv_spec, kv_spec],
        out_specs=pl.BlockSpec((tile, d), lambda b, t: (b * n_t + t, 0)),
        out_shape=jax.ShapeDtypeStruct((n, d), BF16),
        compiler_params=_params(2),
        name="attn",
    )(q, mem_k, mem_v)


def _route(s, sel):
    neg = -jnp.inf

    def top2(vals):
        m1 = functools.reduce(jnp.maximum, vals)
        i1 = jnp.full(m1.shape, len(vals) - 1, jnp.int32)
        for j in range(len(vals) - 2, -1, -1):
            i1 = jnp.where(vals[j] == m1, j, i1)
        rest = [jnp.where(i1 == j, neg, vals[j]) for j in range(len(vals))]
        m2 = functools.reduce(jnp.maximum, rest)
        i2 = jnp.full(m1.shape, len(vals) - 1, jnp.int32)
        for j in range(len(vals) - 2, -1, -1):
            i2 = jnp.where(rest[j] == m2, j, i2)
        return m1, m2, i1, i2

    groups = [top2(sel[g * EXPERTS_PER_GROUP:(g + 1) * EXPERTS_PER_GROUP]) for g in range(N_GROUPS)]
    score = [m1 + m2 for (m1, m2, _, _) in groups]
    best = functools.reduce(jnp.maximum, score)
    g_idx = jnp.full(best.shape, N_GROUPS - 1, jnp.int32)
    for g in range(N_GROUPS - 2, -1, -1):
        g_idx = jnp.where(score[g] == best, g, g_idx)
    e1 = jnp.zeros(best.shape, jnp.int32)
    e2 = jnp.zeros(best.shape, jnp.int32)
    for g in range(N_GROUPS):
        e1 = jnp.where(g_idx == g, g * EXPERTS_PER_GROUP + groups[g][2], e1)
        e2 = jnp.where(g_idx == g, g * EXPERTS_PER_GROUP + groups[g][3], e2)
    w1 = functools.reduce(jnp.add, [jnp.where(e1 == e, s[e], 0.0) for e in range(N_EXPERTS)])
    w2 = functools.reduce(jnp.add, [jnp.where(e2 == e, s[e], 0.0) for e in range(N_EXPERTS)])
    tot = w1 + w2
    w1 = w1 / tot
    w2 = w2 / tot
    return [jnp.where(e1 == e, w1, 0.0) + jnp.where(e2 == e, w2, 0.0) for e in range(N_EXPERTS)]


def _post_kernel(x1_ref, o_ref, wo_ref, lng_ref, lnb_ref, rwt_ref, rb_ref, x2_ref, x2b_ref, comb_ref, *, tile):
    y = jnp.dot(o_ref[...], wo_ref[...], preferred_element_type=F32)
    x2 = _layer_norm(DN_ALPHA * x1_ref[...] + y, lng_ref[...], lnb_ref[...])
    x2_ref[...] = x2
    x2b_ref[...] = x2.astype(BF16)
    logits = lax.dot_general(rwt_ref[...], x2, (((1,), (1,)), ((), ())), preferred_element_type=F32,
                             precision=lax.Precision.HIGHEST)
    s_all = _sigmoid(logits)
    sel_all = s_all + rb_ref[...]
    s = [s_all[e:e + 1] for e in range(N_EXPERTS)]
    sel = [sel_all[e:e + 1] for e in range(N_EXPERTS)]
    comb = jnp.concatenate(_route(s, sel) + [jnp.zeros((LANES - N_EXPERTS, tile), F32)], axis=0)
    for j in range(tile // LANES):
        comb_ref[j * LANES:(j + 1) * LANES, :] = comb[:, j * LANES:(j + 1) * LANES].T


def _post(x1, o, p, router_wt, router_b, tile):
    n, d = x1.shape
    ins = [x1, o, p["x_wo"], p["ln_g"][1].reshape(1, d), p["ln_b"][1].reshape(1, d), router_wt,
           router_b.reshape(N_EXPERTS, 1)]
    in_specs = [pl.BlockSpec((tile, d), lambda i: (i, 0))] * 2 + [_resident(a.shape) for a in ins[2:]]
    return pl.pallas_call(
        functools.partial(_post_kernel, tile=tile),
        grid=(n // tile,),
        in_specs=in_specs,
        out_specs=[pl.BlockSpec((tile, d), lambda i: (i, 0))] * 2 + [pl.BlockSpec((tile, LANES), lambda i: (i, 0))],
        out_shape=[jax.ShapeDtypeStruct((n, d), F32), jax.ShapeDtypeStruct((n, d), BF16),
                   jax.ShapeDtypeStruct((n, LANES), F32)],
        compiler_params=_params(1),
        name="post_attn_router",
    )(*ins)


def _moe_kernel(x2_ref, xb_ref, comb_ref, wg_ref, wu_ref, wd_ref, lng_ref, lnb_ref, out_ref, acc_ref):
    e = pl.program_id(1)

    @pl.when(e == 0)
    def _zero():
        acc_ref[...] = jnp.zeros_like(acc_ref)

    xb = xb_ref[...]
    gate = jnp.dot(xb, wg_ref[...], preferred_element_type=F32)
    up = jnp.dot(xb, wu_ref[...], preferred_element_type=F32)
    hid = gate * _sigmoid(gate) * up
    y = _bdot(hid, wd_ref[...])
    comb = comb_ref[...]
    lane = lax.broadcasted_iota(jnp.int32, comb.shape, 1)
    c_e = jnp.sum(jnp.where(lane == e, comb, 0.0), axis=-1, keepdims=True)
    acc_ref[...] += c_e * y

    @pl.when(e == pl.num_programs(1) - 1)
    def _finish():
        out_ref[...] = _layer_norm(DN_ALPHA * x2_ref[...] + acc_ref[...], lng_ref[...], lnb_ref[...])


def _moe(x2, x2b, comb, p, tile):
    n, d = x2.shape
    de = p["e_wg"].shape[-1]
    row_spec = lambda w: pl.BlockSpec((tile, w), lambda i, e: (i, 0))
    return pl.pallas_call(
        _moe_kernel,
        grid=(n // tile, N_EXPERTS),
        in_specs=[row_spec(d), row_spec(d), row_spec(LANES),
                  pl.BlockSpec((None, d, de), lambda i, e: (e, 0, 0)),
                  pl.BlockSpec((None, d, de), lambda i, e: (e, 0, 0)),
                  pl.BlockSpec((None, de, d), lambda i, e: (e, 0, 0)),
                  _resident((1, d)), _resident((1, d))],
        out_specs=row_spec(d),
        out_shape=jax.ShapeDtypeStruct((n, d), F32),
        scratch_shapes=[pltpu.VMEM((tile, d), F32)],
        compiler_params=_params(2),
        name="moe",
    )(x2, x2b, comb, p["e_wg"], p["e_wu"], p["e_wd"], p["ln_g"][2].reshape(1, d), p["ln_b"][2].reshape(1, d))


def _pick(n, pref):
    return pref if n % pref == 0 else n


def kernel(x_prompt, x_sample, mem_prompt, cache_mem_k, cache_mem_v, state_conv, state_rglru, state_pool, state_hgrn, ln_in_g, ln_in_b, w_in, b_in, sgu_ln_g, sgu_ln_b, sgu_w, sgu_b, conv_w, conv_b, lru_wr, lru_br, lru_wi, lru_bi, lru_lambda, pool_w, pool_b, pool_scale, hgrn_lb, hgrn_norm_g, w_branch, w_out, x_wq, x_wk, x_wv, x_wo, ln_g, ln_b, router_w, router_b, e_wg, e_wu, e_wd):
    bp, tp, d = x_prompt.shape
    bs, ts, _ = x_sample.shape
    depth = w_in.shape[0]
    br = d // 2
    n_mem = mem_prompt.shape[1]
    hd = br // HG_HEADS
    mixw = 9 * br
    assert depth == MODEL_DEPTH

    lb_cum = jnp.cumsum(jax.nn.softmax(hgrn_lb.astype(F32), axis=0), axis=0)
    lbs = lb_cum - lb_cum[0]

    w_in_mix = w_in[:, :, :mixw].astype(BF16)
    w_in_gate = w_in[:, :, mixw:].astype(BF16)
    wts = dict(w_branch=w_branch, w_out=w_out, x_wq=x_wq, x_wo=x_wo, lru_wr=lru_wr, lru_wi=lru_wi, pool_w=pool_w,
               e_wg=e_wg, e_wu=e_wu, e_wd=e_wd)
    wts = {k: v.astype(BF16) for k, v in wts.items()}
    router_wt = router_w.T

    tile_p = SGU_CHUNK if tp % SGU_CHUNK == 0 else tp
    tile_s = SGU_CHUNK if ts % SGU_CHUNK == 0 else ts
    sub_p = _pick(tile_p, 16)
    sub_s = _pick(tile_s, 16)

    n_p, n_s = bp * tp, bs * ts
    xp = _input_ln(x_prompt.reshape(n_p, d), ln_in_g, ln_in_b, _pick(n_p, 512))
    xs = _input_ln(x_sample.reshape(n_s, d), ln_in_g, ln_in_b, _pick(n_s, 512))

    mem_flat = mem_prompt.reshape(bp * n_mem, d)
    mk_all, mv_all = _memory_kv(mem_flat, x_wk.astype(BF16), x_wv.astype(BF16), _pick(bp * n_mem, 512))
    mk_all = mk_all.reshape(depth, bp, n_mem, d)
    mv_all = mv_all.reshape(depth, bp, n_mem, d)
    ck = cache_mem_k.reshape(depth, bs, n_mem, d)
    cv = cache_mem_v.reshape(depth, bs, n_mem, d)

    outs_p, outs_s = [], []
    for l in range(depth):
        p = dict(w_in_mix=w_in_mix[l], w_in_gate=w_in_gate[l], b_in=b_in[l], sgu_ln_g=sgu_ln_g[l],
                 sgu_ln_b=sgu_ln_b[l], sgu_w=sgu_w[l], sgu_b=sgu_b[l], conv_w=conv_w[l], conv_b=conv_b[l],
                 lru_br=lru_br[l], lru_bi=lru_bi[l], lru_lambda=lru_lambda[l], pool_b=pool_b[l],
                 pool_scale=pool_scale[l], hgrn_norm_g=hgrn_norm_g[l], ln_g=ln_g[l], ln_b=ln_b[l])
        p.update({k: v[l] for k, v in wts.items()})

        obr_p, conv_p, h_p, pool_p, s_p = _seqmix(xp, p, lbs[l], None, batch=bp, seq=tp, tile=tile_p, sub=sub_p,
                                                   pos0=0, emit_v=False)
        obr_s, conv_s, h_s, pool_s, s_s, v_s = _seqmix(
            xs, p, lbs[l], (state_conv[l], state_rglru[l], state_pool[l], state_hgrn[l]),
            batch=bs, seq=ts, tile=tile_s, sub=sub_s, pos0=PAST_LEN, emit_v=True)
        outs_p.append((conv_p, h_p.reshape(bp, br), pool_p, s_p))
        outs_s.append((conv_s, h_s.reshape(bs, br), pool_s, s_s, v_s.reshape(bs, ts, br)))

        x1p, qp = _merge(xp, obr_p, p, _pick(n_p, 256))
        x1s, qs = _merge(xs, obr_s, p, _pick(n_s, 256))
        op = _attn(qp, mk_all, mv_all, l, batch=bp, seq=tp, tile=_pick(tp, 256))
        os_ = _attn(qs, ck, cv, l, batch=bs, seq=ts, tile=ts)
        x2p, x2pb, comb_p = _post(x1p, op, p, router_wt, router_b, _pick(n_p, 256))
        x2s, x2sb, comb_s = _post(x1s, os_, p, router_wt, router_b, _pick(n_s, 256))
        xp = _moe(x2p, x2pb, comb_p, p, _pick(n_p, 1024))
        xs = _moe(x2s, x2sb, comb_s, p, _pick(n_s, 1024))

    stack = lambda outs, i: jnp.stack([o[i] for o in outs])
    mk_out = mk_all.reshape(depth, bp, n_mem, X_HEADS, d // X_HEADS)
    mv_out = mv_all.reshape(depth, bp, n_mem, X_HEADS, d // X_HEADS)
    return (xp.reshape(bp, tp, d), xs.reshape(bs, ts, d), mk_out, mv_out,
            stack(outs_p, 0), stack(outs_p, 1), stack(outs_p, 2), stack(outs_p, 3),
            stack(outs_s, 0), stack(outs_s, 1), stack(outs_s, 2), stack(outs_s, 3), stack(outs_s, 4))
```

```python
import functools

import jax
import jax.numpy as jnp
from jax import lax
from jax.experimental import pallas as pl
from jax.experimental.pallas import tpu as pltpu

F32 = jnp.float32
BF16 = jnp.bfloat16

N_BRANCH = 4
SGU_GROUPS = 4
SGU_CHUNK = 128
LRU_HEADS = 4
CONV_W = 4
LRU_C = 8.0
POOL_WINDOWS = (2, 4, 8, 16)
POOL_BUF = max(POOL_WINDOWS) - 1
HG_HEADS = 4
X_HEADS = 4
N_EXPERTS = 16
N_GROUPS = 4
EXPERTS_PER_GROUP = N_EXPERTS // N_GROUPS
MODEL_DEPTH = 4
DN_ALPHA = (2 * MODEL_DEPTH) ** 0.25
LN_EPS = 1e-5
RMS_EPS = 1e-6
PAST_LEN = 16384
MOE_TILE = 1024
MOE_CHUNK = 128
SEG_ALIGN = 16

V7X_VMEM_BYTES = 64 * 1024 * 1024
VMEM_LIMIT = V7X_VMEM_BYTES - 8 * 1024 * 1024
SUBLANES = 8
LANES = 128


def _params(n_grid):
    return pltpu.CompilerParams(dimension_semantics=("arbitrary",) * n_grid, vmem_limit_bytes=VMEM_LIMIT)


def _resident(shape):
    nd = len(shape)
    return pl.BlockSpec(shape, lambda *_: (0,) * nd, pipeline_mode=pl.Buffered(1))


def _layer_norm(x, g, b):
    mu = jnp.mean(x, axis=-1, keepdims=True)
    xc = x - mu
    var = jnp.mean(xc * xc, axis=-1, keepdims=True)
    return xc * lax.rsqrt(var + LN_EPS) * g + b


def _sigmoid(x):
    return 1.0 / (1.0 + jnp.exp(-x))


def _bdot(a, b):
    return jnp.dot(a.astype(BF16), b.astype(BF16), preferred_element_type=F32)


def _bdot_nt(a, b):
    return lax.dot_general(a.astype(BF16), b.astype(BF16), (((1,), (1,)), ((), ())), preferred_element_type=F32)


def _bdot_tn(a, b):
    return lax.dot_general(a.astype(BF16), b.astype(BF16), (((0,), (0,)), ((), ())), preferred_element_type=F32)


def _shift_rows(x, k):
    return pltpu.roll(x, k, 0)


def _ln_kernel(x_ref, g_ref, b_ref, o_ref):
    o_ref[...] = _layer_norm(x_ref[...], g_ref[...], b_ref[...])


def _input_ln(x, g, b, tile):
    n, d = x.shape
    return pl.pallas_call(
        _ln_kernel,
        grid=(n // tile,),
        in_specs=[pl.BlockSpec((tile, d), lambda i: (i, 0)), _resident((1, d)), _resident((1, d))],
        out_specs=pl.BlockSpec((tile, d), lambda i: (i, 0)),
        out_shape=jax.ShapeDtypeStruct((n, d), F32),
        compiler_params=_params(1),
        name="input_ln",
    )(x, g.reshape(1, d), b.reshape(1, d))


def _kv_kernel(m_ref, wk_ref, wv_ref, k_ref, v_ref, *, hd):
    m = m_ref[...].astype(BF16)
    k = jnp.dot(m, wk_ref[...], preferred_element_type=F32)
    v = jnp.dot(m, wv_ref[...], preferred_element_type=F32)
    for h in range(X_HEADS):
        k_ref[:, h, :] = k[:, h * hd:(h + 1) * hd]
        v_ref[:, h, :] = v[:, h * hd:(h + 1) * hd]


def _memory_kv(mem, wk, wv, tile):
    n, d = mem.shape
    depth = wk.shape[0]
    hd = d // X_HEADS
    out = jax.ShapeDtypeStruct((depth, n, X_HEADS, hd), F32)
    return pl.pallas_call(
        functools.partial(_kv_kernel, hd=hd),
        grid=(depth, n // tile),
        in_specs=[
            pl.BlockSpec((tile, d), lambda l, i: (i, 0)),
            pl.BlockSpec((None, d, d), lambda l, i: (l, 0, 0)),
            pl.BlockSpec((None, d, d), lambda l, i: (l, 0, 0)),
        ],
        out_specs=[pl.BlockSpec((None, tile, X_HEADS, hd), lambda l, i: (l, i, 0, 0))] * 2,
        out_shape=[out, out],
        compiler_params=_params(2),
        name="memory_kv",
    )(mem, wk, wv)


def _seqmix_kernel(*refs, tile, sub, pos0, has_state, emit_v, br):
    (x_ref, win_ref, bin_ref, sgug_ref, sgub_ref, sguw_ref, sgubt_ref, convw_ref, convb_ref, wr_ref, br_ref,
     wi_ref, bi_ref, lam_ref, poolw_ref, poolb_ref, pools_ref, lb_ref, hgn_ref) = refs[:19]
    refs = refs[19:]
    if has_state:
        conv0_ref, h0_ref, pool0_ref, s0_ref = refs[:4]
        refs = refs[4:]
    obr_ref, convn_ref, hn_ref, pooln_ref, sn_ref = refs[:5]
    refs = refs[5:]
    if emit_v:
        vrows_ref = refs[0]
        refs = refs[1:]
    conv_sc, pool_sc, h_sc, st_sc = refs

    t = pl.program_id(1)
    n_t = pl.num_programs(1)
    gd = br // SGU_GROUPS
    hd = br // HG_HEADS

    @pl.when(t == 0)
    def _init():
        conv_sc[...] = jnp.zeros_like(conv_sc)
        pool_sc[...] = jnp.zeros_like(pool_sc)
        if has_state:
            conv_sc[SUBLANES - (CONV_W - 1):, :] = conv0_ref[...]
            pool_sc[1:, :] = pool0_ref[...]
            h_sc[...] = h0_ref[...]
            for h in range(HG_HEADS):
                st_sc[h] = s0_ref[h].T
        else:
            h_sc[...] = jnp.zeros_like(h_sc)
            st_sc[...] = jnp.zeros_like(st_sc)

    x = x_ref[...]
    z = jnp.dot(x.astype(BF16), win_ref[...], preferred_element_type=F32) + bin_ref[...]

    def col(k):
        return z[:, k * br:(k + 1) * br]

    u, v, yg, xb, c, q, f, vi, go = [col(k) for k in range(9)]
    row = lax.broadcasted_iota(jnp.int32, (tile, br), 0)

    vn = _layer_norm(v, sgug_ref[...], sgub_ref[...])
    if emit_v:
        vrows_ref[...] = vn
    rr = lax.broadcasted_iota(jnp.int32, (tile, tile), 0)
    cc = lax.broadcasted_iota(jnp.int32, (tile, tile), 1)
    o_a = []
    for g in range(SGU_GROUPS):
        w = jnp.where(rr >= cc, sguw_ref[g, :tile, :tile], 0.0)
        mixed = _bdot(w, vn[:, g * gd:(g + 1) * gd]) + sgubt_ref[:tile, g:g + 1]
        o_a.append(u[:, g * gd:(g + 1) * gd] * mixed)
    o_a = jnp.concatenate(o_a, axis=1)

    ext = jnp.concatenate([conv_sc[...], xb], axis=0)
    cw = convw_ref[...]
    base = SUBLANES - (CONV_W - 1)
    xc = convb_ref[...] + xb * cw[CONV_W - 1:CONV_W]
    for j in range(CONV_W - 1):
        xc = xc + ext[base + j:base + j + tile] * cw[j:j + 1]
    conv_sc[...] = ext[tile:tile + SUBLANES]
    convn_ref[...] = ext[tile + base:tile + SUBLANES]

    def heads_dot(a, w_ref):
        return jnp.concatenate([_bdot(a[:, h * hd:(h + 1) * hd], w_ref[h]) for h in range(LRU_HEADS)], axis=1)

    r_g = _sigmoid(heads_dot(xc, wr_ref) + br_ref[...])
    i_g = _sigmoid(heads_dot(xc, wi_ref) + bi_ref[...])
    lam = lam_ref[...]
    softplus_neg_lam = jnp.maximum(-lam, 0.0) + jnp.log1p(jnp.exp(-jnp.abs(lam)))
    log_a = (-LRU_C) * r_g * softplus_neg_lam
    a = jnp.exp(log_a)
    y2 = 2.0 * log_a
    one_minus = jnp.tanh(-0.5 * y2) * (jnp.exp(y2) + 1.0)
    bx = jnp.sqrt(one_minus) * (i_g * xc)
    sa, sb = a, bx
    k = 1
    while k < tile:
        keep = row >= k
        sb = sa * jnp.where(keep, _shift_rows(sb, k), 0.0) + sb
        sa = sa * jnp.where(keep, _shift_rows(sa, k), 1.0)
        k *= 2
    hseq = sb + sa * h_sc[...]
    h_sc[...] = hseq[tile - 1:tile]
    hn_ref[...] = hseq[tile - 1:tile]
    gelu = 0.5 * yg * (1.0 + jnp.tanh(0.7978845608028654 * (yg + 0.044715 * (yg * yg * yg))))
    o_b = gelu * hseq

    extp = jnp.concatenate([pool_sc[...], c], axis=0)
    pb = POOL_BUF + 1
    sums = {1: extp}
    wdt = 1
    while wdt < max(POOL_WINDOWS):
        sums[2 * wdt] = sums[wdt] + _shift_rows(sums[wdt], wdt)
        wdt *= 2
    tpos = pos0 + 1 + t * tile + lax.broadcasted_iota(jnp.int32, (tile, 1), 0)
    o_c = []
    for g, wdw in enumerate(POOL_WINDOWS):
        cnt = jnp.minimum(wdw, tpos).astype(F32)
        pooled = sums[wdw][pb:pb + tile, g * gd:(g + 1) * gd] / cnt
        o_c.append(_bdot(pooled - c[:, g * gd:(g + 1) * gd], poolw_ref[g]))
    o_c = (jnp.concatenate(o_c, axis=1) + poolb_ref[...]) * pools_ref[...]
    pool_sc[...] = extp[tile:tile + pb]
    pooln_ref[...] = extp[tile + 1:tile + pb]

    lb = lb_ref[...]
    log_sig = jnp.minimum(f, 0.0) - jnp.log1p(jnp.exp(-jnp.abs(f)))
    a1 = jnp.log(lb)
    a2 = jnp.log1p(-lb) + log_sig
    log_f = jnp.maximum(a1, a2) + jnp.log1p(jnp.exp(-jnp.abs(a1 - a2)))
    kf = (1.0 - lb) * _sigmoid(-f)
    qf = q * _sigmoid(q)
    rmod = row % sub
    bc = log_f
    k = 1
    while k < sub:
        bc = bc + jnp.where(rmod >= k, _shift_rows(bc, k), 0.0)
        k *= 2
    o_d = [jnp.zeros((tile, hd), F32) for _ in range(HG_HEADS)]
    for d in range(sub):
        if d == 0:
            p = qf * kf
            vd = vi
        else:
            diff = jnp.where(rmod >= d, bc - _shift_rows(bc, d), -jnp.inf)
            p = qf * _shift_rows(kf, d) * jnp.exp(diff)
            vd = _shift_rows(vi, d)
        for h in range(HG_HEADS):
            att = jnp.sum(p[:, h * hd:(h + 1) * hd], axis=-1, keepdims=True)
            o_d[h] = o_d[h] + att * vd[:, h * hd:(h + 1) * hd]
    qe = qf * jnp.exp(bc)
    for j in range(tile // sub):
        lo, hi = j * sub, (j + 1) * sub
        b_last = bc[hi - 1:hi]
        kdec = kf[lo:hi] * jnp.exp(b_last - bc[lo:hi])
        decay = jnp.exp(b_last)
        inter = []
        for h in range(HG_HEADS):
            st = st_sc[h]
            inter.append(_bdot_nt(qe[lo:hi, h * hd:(h + 1) * hd], st))
            upd = _bdot_tn(vi[lo:hi, h * hd:(h + 1) * hd], kdec[:, h * hd:(h + 1) * hd])
            st_sc[h] = st * decay[:, h * hd:(h + 1) * hd] + upd
        inter = jnp.concatenate(inter, axis=1)
        if j == 0:
            o_inter = [inter]
        else:
            o_inter.append(inter)
    o_inter = jnp.concatenate(o_inter, axis=0) if len(o_inter) > 1 else o_inter[0]
    gn = hgn_ref[...]
    o_n = []
    for h in range(HG_HEADS):
        oh = o_d[h] + o_inter[:, h * hd:(h + 1) * hd]
        ms = jnp.mean(oh * oh, axis=-1, keepdims=True)
        o_n.append(oh * lax.rsqrt(ms + RMS_EPS) * gn)
    o_d = jnp.concatenate(o_n, axis=1) * (go * _sigmoid(go))

    obr_ref[...] = jnp.concatenate([o_a, o_b, o_c, o_d], axis=1).astype(BF16)

    @pl.when(t == n_t - 1)
    def _emit_state():
        for h in range(HG_HEADS):
            sn_ref[h] = st_sc[h].T


def _seqmix(x, p, lb, states, *, batch, seq, tile, sub, pos0, emit_v):
    n, d = x.shape
    br = d // 2
    hd = br // HG_HEADS
    n_t = seq // tile
    has_state = states is not None
    mixw = 9 * br
    row = lambda a: a.reshape(1, -1)
    ins = [x, p["w_in_mix"], row(p["b_in"][:mixw]), row(p["sgu_ln_g"]), row(p["sgu_ln_b"]), p["sgu_w"], p["sgu_b"].T,
           p["conv_w"], row(p["conv_b"]), p["lru_wr"], row(p["lru_br"]), p["lru_wi"], row(p["lru_bi"]),
           row(p["lru_lambda"]), p["pool_w"], row(p["pool_b"]), row(p["pool_scale"]), row(lb), row(p["hgrn_norm_g"])]
    in_specs = [pl.BlockSpec((tile, d), lambda b, t: (b * n_t + t, 0))] + [_resident(a.shape) for a in ins[1:]]
    if has_state:
        conv0, h0, pool0, s0 = states
        ins += [conv0, h0.reshape(batch, 1, br), pool0, s0]
        in_specs += [
            pl.BlockSpec((None, CONV_W - 1, br), lambda b, t: (b, 0, 0)),
            pl.BlockSpec((None, 1, br), lambda b, t: (b, 0, 0)),
            pl.BlockSpec((None, POOL_BUF, br), lambda b, t: (b, 0, 0)),
            pl.BlockSpec((None, HG_HEADS, hd, hd), lambda b, t: (b, 0, 0, 0)),
        ]
    out_shape = [
        jax.ShapeDtypeStruct((n, N_BRANCH * br), BF16),
        jax.ShapeDtypeStruct((batch, CONV_W - 1, br), F32),
        jax.ShapeDtypeStruct((batch, 1, br), F32),
        jax.ShapeDtypeStruct((batch, POOL_BUF, br), F32),
        jax.ShapeDtypeStruct((batch, HG_HEADS, hd, hd), F32),
    ]
    out_specs = [
        pl.BlockSpec((tile, N_BRANCH * br), lambda b, t: (b * n_t + t, 0)),
        pl.BlockSpec((None, CONV_W - 1, br), lambda b, t: (b, 0, 0)),
        pl.BlockSpec((None, 1, br), lambda b, t: (b, 0, 0)),
        pl.BlockSpec((None, POOL_BUF, br), lambda b, t: (b, 0, 0)),
        pl.BlockSpec((None, HG_HEADS, hd, hd), lambda b, t: (b, 0, 0, 0)),
    ]
    if emit_v:
        out_shape.append(jax.ShapeDtypeStruct((n, br), F32))
        out_specs.append(pl.BlockSpec((tile, br), lambda b, t: (b * n_t + t, 0)))
    scratch = [
        pltpu.VMEM((SUBLANES, br), F32),
        pltpu.VMEM((POOL_BUF + 1, br), F32),
        pltpu.VMEM((1, br), F32),
        pltpu.VMEM((HG_HEADS, hd, hd), F32),
    ]
    kern = functools.partial(_seqmix_kernel, tile=tile, sub=sub, pos0=pos0, has_state=has_state, emit_v=emit_v, br=br)
    outs = pl.pallas_call(
        kern,
        grid=(batch, n_t),
        in_specs=in_specs,
        out_specs=out_specs,
        out_shape=out_shape,
        scratch_shapes=scratch,
        compiler_params=_params(2),
        name="seqmix_state" if has_state else "seqmix",
    )(*ins)
    return outs


def _merge_kernel(x_ref, obr_ref, wg_ref, bg_ref, wb_ref, wo_ref, lng_ref, lnb_ref, wq_ref, x1_ref, q_ref, *, br, d):
    x = x_ref[...]
    xb16 = x.astype(BF16)
    acc = None
    for g in range(N_BRANCH):
        gz = jnp.dot(xb16, wg_ref[:, g * d:(g + 1) * d], preferred_element_type=F32) + bg_ref[:, g * d:(g + 1) * d]
        proj = jnp.dot(obr_ref[:, g * br:(g + 1) * br], wb_ref[g], preferred_element_type=F32)
        term = _sigmoid(gz) * proj
        acc = term if acc is None else acc + term
    y = _bdot(acc, wo_ref[...])
    x1 = _layer_norm(DN_ALPHA * x + y, lng_ref[...], lnb_ref[...])
    x1_ref[...] = x1
    q_ref[...] = _bdot(x1, wq_ref[...]).astype(BF16)


def _merge(x, obr, p, tile):
    n, d = x.shape
    br = d // 2
    ins = [x, obr, p["w_in_gate"], p["b_in"][9 * br:].reshape(1, -1), p["w_branch"], p["w_out"],
           p["ln_g"][0].reshape(1, d), p["ln_b"][0].reshape(1, d), p["x_wq"]]
    in_specs = [pl.BlockSpec((tile, d), lambda i: (i, 0)), pl.BlockSpec((tile, N_BRANCH * br), lambda i: (i, 0))]
    in_specs += [_resident(a.shape) for a in ins[2:]]
    return pl.pallas_call(
        functools.partial(_merge_kernel, br=br, d=d),
        grid=(n // tile,),
        in_specs=in_specs,
        out_specs=[pl.BlockSpec((tile, d), lambda i: (i, 0))] * 2,
        out_shape=[jax.ShapeDtypeStruct((n, d), F32), jax.ShapeDtypeStruct((n, d), BF16)],
        compiler_params=_params(1),
        name="merge",
    )(*ins)


def _attn_kernel(q_ref, k_ref, v_ref, o_ref, *, hd):
    scale = hd ** -0.5
    outs = []
    for h in range(X_HEADS):
        s = _bdot_nt(q_ref[:, h * hd:(h + 1) * hd], k_ref[:, h, :]) * scale
        e = jnp.exp(s - jnp.max(s, axis=-1, keepdims=True))
        prob = e / jnp.sum(e, axis=-1, keepdims=True)
        outs.append(_bdot(prob, v_ref[:, h, :]))
    o_ref[...] = jnp.concatenate(outs, axis=1).astype(BF16)


def _attn(q, mem_k, mem_v, layer, *, batch, seq, tile):
    n, d = q.shape
    n_t = seq // tile
    m, hd = mem_k.shape[2], mem_k.shape[4]
    kv_spec = pl.BlockSpec((None, None, m, X_HEADS, hd), lambda b, t: (layer, b, 0, 0, 0))
    return pl.pallas_call(
        functools.partial(_attn_kernel, hd=hd),
        grid=(batch, n_t),
        in_specs=[pl.BlockSpec((tile, d), lambda b, t: (b * n_t + t, 0)), kv_spec, kv_spec],
        out_specs=pl.BlockSpec((tile, d), lambda b, t: (b * n_t + t, 0)),
        out_shape=jax.ShapeDtypeStruct((n, d), BF16),
        compiler_params=_params(2),
        name="attn",
    )(q, mem_k, mem_v)


def _route(s, sel):
    neg = -jnp.inf

    def top2(vals):
        m1 = functools.reduce(jnp.maximum, vals)
        i1 = jnp.full(m1.shape, len(vals) - 1, jnp.int32)
        for j in range(len(vals) - 2, -1, -1):
            i1 = jnp.where(vals[j] == m1, j, i1)
        rest = [jnp.where(i1 == j, neg, vals[j]) for j in range(len(vals))]
        m2 = functools.reduce(jnp.maximum, rest)
        i2 = jnp.full(m1.shape, len(vals) - 1, jnp.int32)
        for j in range(len(vals) - 2, -1, -1):
            i2 = jnp.where(rest[j] == m2, j, i2)
        return m1, m2, i1, i2

    groups = [top2(sel[g * EXPERTS_PER_GROUP:(g + 1) * EXPERTS_PER_GROUP]) for g in range(N_GROUPS)]
    score = [m1 + m2 for (m1, m2, _, _) in groups]
    best = functools.reduce(jnp.maximum, score)
    g_idx = jnp.full(best.shape, N_GROUPS - 1, jnp.int32)
    for g in range(N_GROUPS - 2, -1, -1):
        g_idx = jnp.where(score[g] == best, g, g_idx)
    e1 = jnp.zeros(best.shape, jnp.int32)
    e2 = jnp.zeros(best.shape, jnp.int32)
    for g in range(N_GROUPS):
        e1 = jnp.where(g_idx == g, g * EXPERTS_PER_GROUP + groups[g][2], e1)
        e2 = jnp.where(g_idx == g, g * EXPERTS_PER_GROUP + groups[g][3], e2)
    w1 = functools.reduce(jnp.add, [jnp.where(e1 == e, s[e], 0.0) for e in range(N_EXPERTS)])
    w2 = functools.reduce(jnp.add, [jnp.where(e2 == e, s[e], 0.0) for e in range(N_EXPERTS)])
    tot = w1 + w2
    return e1, e2, w1 / tot, w2 / tot


def _post_kernel(x1_ref, o_ref, wo_ref, lng_ref, lnb_ref, rwt_ref, rb_ref, tri_ref,
                 x2_ref, x2b_ref, rows_ref, cols_ref, seg_ref, *, tile):
    y = jnp.dot(o_ref[...], wo_ref[...], preferred_element_type=F32)
    x2 = _layer_norm(DN_ALPHA * x1_ref[...] + y, lng_ref[...], lnb_ref[...])
    x2_ref[...] = x2
    x2b_ref[...] = x2.astype(BF16)
    logits = lax.dot_general(rwt_ref[...], x2, (((1,), (1,)), ((), ())), preferred_element_type=F32,
                             precision=lax.Precision.HIGHEST)
    s_all = _sigmoid(logits)
    sel_all = s_all + rb_ref[...]
    s = [s_all[e:e + 1] for e in range(N_EXPERTS)]
    sel = [sel_all[e:e + 1] for e in range(N_EXPERTS)]
    e1, e2, w1, w2 = _route(s, sel)

    onehot = jnp.concatenate([jnp.where((e1 == e) | (e2 == e), 1.0, 0.0) for e in range(N_EXPERTS)], axis=0)
    prefix = jnp.dot(onehot.astype(BF16), tri_ref[...], preferred_element_type=F32)
    cnt = jnp.broadcast_to(jnp.sum(onehot, axis=-1, keepdims=True), (N_EXPERTS, LANES))
    cpad = jnp.floor((cnt + (SEG_ALIGN - 1.0)) * (1.0 / SEG_ALIGN)) * SEG_ALIGN
    ridx = lax.broadcasted_iota(jnp.int32, (N_EXPERTS, LANES), 0)
    inc = cpad
    k = 1
    while k < N_EXPERTS:
        inc = inc + jnp.where(ridx >= k, _shift_rows(inc, k), 0.0)
        k *= 2
    start = inc - cpad

    def pick(e_row, table):
        return functools.reduce(jnp.add, [jnp.where(e_row == e, table[e:e + 1], 0.0) for e in range(N_EXPERTS)])

    start_col = start[:, 0:1]
    pos1 = pick(e1, start_col) + pick(e1, prefix)
    pos2 = pick(e2, start_col) + pick(e2, prefix)
    rows = jnp.concatenate([pos1, pos2, w1, w2, jnp.zeros((SUBLANES - 4, tile), F32)], axis=0)
    rows_ref[...] = rows
    padded = jnp.concatenate([rows, jnp.zeros((LANES - SUBLANES, tile), F32)], axis=0)
    for j in range(tile // LANES):
        cols_ref[j * LANES:(j + 1) * LANES, :] = padded[:, j * LANES:(j + 1) * LANES].T
    seg_ref[0] = start
    seg_ref[1] = cnt


def _post(x1, o, p, router_wt, router_b, tri, tile):
    n, d = x1.shape
    ins = [x1, o, p["x_wo"], p["ln_g"][1].reshape(1, d), p["ln_b"][1].reshape(1, d), router_wt,
           router_b.reshape(N_EXPERTS, 1), tri]
    in_specs = [pl.BlockSpec((tile, d), lambda i: (i, 0))] * 2 + [_resident(a.shape) for a in ins[2:]]
    return pl.pallas_call(
        functools.partial(_post_kernel, tile=tile),
        grid=(n // tile,),
        in_specs=in_specs,
        out_specs=[pl.BlockSpec((tile, d), lambda i: (i, 0))] * 2 + [
            pl.BlockSpec((SUBLANES, tile), lambda i: (0, i)),
            pl.BlockSpec((tile, LANES), lambda i: (i, 0)),
            pl.BlockSpec((None, 2, N_EXPERTS, LANES), lambda i: (i, 0, 0, 0))],
        out_shape=[jax.ShapeDtypeStruct((n, d), F32), jax.ShapeDtypeStruct((n, d), BF16),
                   jax.ShapeDtypeStruct((SUBLANES, n), F32), jax.ShapeDtypeStruct((n, LANES), F32),
                   jax.ShapeDtypeStruct((n // tile, 2, N_EXPERTS, LANES), F32)],
        compiler_params=_params(1),
        name="post_attn_router",
    )(*ins)


def _moe_kernel(start_ref, cnt_ref, x2_ref, xb_ref, rows_ref, cols_ref, wg_ref, wu_ref, wd_ref, lng_ref, lnb_ref,
                out_ref, xs_ref, ys_ref, ws_ref, *, tile, rcap, blk):
    i = pl.program_id(0)
    e = pl.program_id(1)

    @pl.when(e == 0)
    def _sort_rows():
        pos1, pos2 = rows_ref[0:1, :], rows_ref[1:2, :]
        w1, w2 = rows_ref[2:3, :], rows_ref[3:4, :]
        xb = xb_ref[...]
        for c in range(rcap // blk):
            r = (c * blk + lax.broadcasted_iota(jnp.int32, (blk, tile), 0)).astype(F32)
            m1 = pos1 == r
            m2 = pos2 == r
            sel = jnp.where(m1 | m2, 1.0, 0.0).astype(BF16)
            xs_ref[c * blk:(c + 1) * blk, :] = jnp.dot(sel, xb, preferred_element_type=F32).astype(BF16)
            ws_ref[c * blk:(c + 1) * blk, :] = jnp.sum(jnp.where(m1, w1, 0.0) + jnp.where(m2, w2, 0.0),
                                                       axis=-1, keepdims=True)
        ys_ref[...] = jnp.zeros_like(ys_ref)

    start = start_ref[i * N_EXPERTS + e]
    n_chunks = lax.shift_right_logical(cnt_ref[i * N_EXPERTS + e] + (MOE_CHUNK - 1), MOE_CHUNK.bit_length() - 1)

    def chunk_body(j, carry):
        r = pl.multiple_of(start + j * MOE_CHUNK, SEG_ALIGN)
        xs = xs_ref[pl.ds(r, MOE_CHUNK), :]
        gate = jnp.dot(xs, wg_ref[...], preferred_element_type=F32)
        up = jnp.dot(xs, wu_ref[...], preferred_element_type=F32)
        hid = gate * _sigmoid(gate) * up
        y = _bdot(hid, wd_ref[...])
        ys_ref[pl.ds(r, MOE_CHUNK), :] = (ws_ref[pl.ds(r, MOE_CHUNK), :] * y).astype(BF16)
        return carry

    lax.fori_loop(0, n_chunks, chunk_body, 0)

    @pl.when(e == pl.num_programs(1) - 1)
    def _finish():
        for c in range(tile // blk):
            sl = slice(c * blk, (c + 1) * blk)
            r = lax.broadcasted_iota(jnp.int32, (blk, rcap), 1).astype(F32)
            back = jnp.where((cols_ref[sl, 0:1] == r) | (cols_ref[sl, 1:2] == r), 1.0, 0.0).astype(BF16)
            y = jnp.dot(back, ys_ref[...], preferred_element_type=F32)
            out_ref[sl, :] = _layer_norm(DN_ALPHA * x2_ref[sl, :] + y, lng_ref[...], lnb_ref[...])


def _moe(x2, x2b, rows, cols, seg, p, tile):
    n, d = x2.shape
    de = p["e_wg"].shape[-1]
    blk = min(256, tile)
    rcap = -(-(2 * tile + N_EXPERTS * SEG_ALIGN + MOE_CHUNK) // blk) * blk
    seg_start = seg[:, 0, :, 0].astype(jnp.int32).reshape(-1)
    seg_cnt = seg[:, 1, :, 0].astype(jnp.int32).reshape(-1)
    row_spec = lambda w: pl.BlockSpec((tile, w), lambda i, e, s, c: (i, 0))
    grid_spec = pltpu.PrefetchScalarGridSpec(
        num_scalar_prefetch=2,
        grid=(n // tile, N_EXPERTS),
        in_specs=[row_spec(d), row_spec(d),
                  pl.BlockSpec((SUBLANES, tile), lambda i, e, s, c: (0, i)),
                  row_spec(LANES),
                  pl.BlockSpec((None, d, de), lambda i, e, s, c: (e, 0, 0)),
                  pl.BlockSpec((None, d, de), lambda i, e, s, c: (e, 0, 0)),
                  pl.BlockSpec((None, de, d), lambda i, e, s, c: (e, 0, 0)),
                  pl.BlockSpec((1, d), lambda i, e, s, c: (0, 0)),
                  pl.BlockSpec((1, d), lambda i, e, s, c: (0, 0))],
        out_specs=row_spec(d),
        scratch_shapes=[pltpu.VMEM((rcap, d), BF16), pltpu.VMEM((rcap, d), BF16), pltpu.VMEM((rcap, 1), F32)],
    )
    return pl.pallas_call(
        functools.partial(_moe_kernel, tile=tile, rcap=rcap, blk=blk),
        grid_spec=grid_spec,
        out_shape=jax.ShapeDtypeStruct((n, d), F32),
        compiler_params=_params(2),
        name="moe",
    )(seg_start, seg_cnt, x2, x2b, rows, cols, p["e_wg"], p["e_wu"], p["e_wd"],
      p["ln_g"][2].reshape(1, d), p["ln_b"][2].reshape(1, d))


def _pick(n, pref):
    return pref if n % pref == 0 else n


def kernel(x_prompt, x_sample, mem_prompt, cache_mem_k, cache_mem_v, state_conv, state_rglru, state_pool, state_hgrn, ln_in_g, ln_in_b, w_in, b_in, sgu_ln_g, sgu_ln_b, sgu_w, sgu_b, conv_w, conv_b, lru_wr, lru_br, lru_wi, lru_bi, lru_lambda, pool_w, pool_b, pool_scale, hgrn_lb, hgrn_norm_g, w_branch, w_out, x_wq, x_wk, x_wv, x_wo, ln_g, ln_b, router_w, router_b, e_wg, e_wu, e_wd):
    bp, tp, d = x_prompt.shape
    bs, ts, _ = x_sample.shape
    depth = w_in.shape[0]
    br = d // 2
    n_mem = mem_prompt.shape[1]
    hd = br // HG_HEADS
    mixw = 9 * br
    assert depth == MODEL_DEPTH

    lb_cum = jnp.cumsum(jax.nn.softmax(hgrn_lb.astype(F32), axis=0), axis=0)
    lbs = lb_cum - lb_cum[0]

    w_in_mix = w_in[:, :, :mixw].astype(BF16)
    w_in_gate = w_in[:, :, mixw:].astype(BF16)
    wts = dict(w_branch=w_branch, w_out=w_out, x_wq=x_wq, x_wo=x_wo, lru_wr=lru_wr, lru_wi=lru_wi, pool_w=pool_w,
               e_wg=e_wg, e_wu=e_wu, e_wd=e_wd)
    wts = {k: v.astype(BF16) for k, v in wts.items()}
    router_wt = router_w.T

    tile_p = SGU_CHUNK if tp % SGU_CHUNK == 0 else tp
    tile_s = SGU_CHUNK if ts % SGU_CHUNK == 0 else ts
    sub_p = _pick(tile_p, 16)
    sub_s = _pick(tile_s, 16)

    n_p, n_s = bp * tp, bs * ts
    xp = _input_ln(x_prompt.reshape(n_p, d), ln_in_g, ln_in_b, _pick(n_p, 512))
    xs = _input_ln(x_sample.reshape(n_s, d), ln_in_g, ln_in_b, _pick(n_s, 512))

    mem_flat = mem_prompt.reshape(bp * n_mem, d)
    mk_all, mv_all = _memory_kv(mem_flat, x_wk.astype(BF16), x_wv.astype(BF16), _pick(bp * n_mem, 512))
    mk_all = mk_all.reshape(depth, bp, n_mem, X_HEADS, d // X_HEADS)
    mv_all = mv_all.reshape(depth, bp, n_mem, X_HEADS, d // X_HEADS)
    ck, cv = cache_mem_k, cache_mem_v

    moe_tile_p = _pick(n_p, MOE_TILE)
    moe_tile_s = _pick(n_s, MOE_TILE)

    def strict_upper(t):
        idx = jnp.arange(t)
        return (idx[:, None] < idx[None, :]).astype(BF16)

    tri_p, tri_s = strict_upper(moe_tile_p), strict_upper(moe_tile_s)

    outs_p, outs_s = [], []
    for l in range(depth):
        p = dict(w_in_mix=w_in_mix[l], w_in_gate=w_in_gate[l], b_in=b_in[l], sgu_ln_g=sgu_ln_g[l],
                 sgu_ln_b=sgu_ln_b[l], sgu_w=sgu_w[l], sgu_b=sgu_b[l], conv_w=conv_w[l], conv_b=conv_b[l],
                 lru_br=lru_br[l], lru_bi=lru_bi[l], lru_lambda=lru_lambda[l], pool_b=pool_b[l],
                 pool_scale=pool_scale[l], hgrn_norm_g=hgrn_norm_g[l], ln_g=ln_g[l], ln_b=ln_b[l])
        p.update({k: v[l] for k, v in wts.items()})

        obr_p, conv_p, h_p, pool_p, s_p = _seqmix(xp, p, lbs[l], None, batch=bp, seq=tp, tile=tile_p, sub=sub_p,
                                                   pos0=0, emit_v=False)
        obr_s, conv_s, h_s, pool_s, s_s, v_s = _seqmix(
            xs, p, lbs[l], (state_conv[l], state_rglru[l], state_pool[l], state_hgrn[l]),
            batch=bs, seq=ts, tile=tile_s, sub=sub_s, pos0=PAST_LEN, emit_v=True)
        outs_p.append((conv_p, h_p.reshape(bp, br), pool_p, s_p))
        outs_s.append((conv_s, h_s.reshape(bs, br), pool_s, s_s, v_s.reshape(bs, ts, br)))

        x1p, qp = _merge(xp, obr_p, p, _pick(n_p, 256))
        x1s, qs = _merge(xs, obr_s, p, _pick(n_s, 256))
        op = _attn(qp, mk_all, mv_all, l, batch=bp, seq=tp, tile=_pick(tp, 256))
        os_ = _attn(qs, ck, cv, l, batch=bs, seq=ts, tile=ts)
        x2p, x2pb, rows_p, cols_p, seg_p = _post(x1p, op, p, router_wt, router_b, tri_p, moe_tile_p)
        x2s, x2sb, rows_s, cols_s, seg_s = _post(x1s, os_, p, router_wt, router_b, tri_s, moe_tile_s)
        xp = _moe(x2p, x2pb, rows_p, cols_p, seg_p, p, moe_tile_p)
        xs = _moe(x2s, x2sb, rows_s, cols_s, seg_s, p, moe_tile_s)

    stack = lambda outs, i: jnp.stack([o[i] for o in outs])
    return (xp.reshape(bp, tp, d), xs.reshape(bs, ts, d), mk_all, mv_all,
            stack(outs_p, 0), stack(outs_p, 1), stack(outs_p, 2), stack(outs_p, 3),
            stack(outs_s, 0), stack(outs_s, 1), stack(outs_s, 2), stack(outs_s, 3), stack(outs_s, 4))
```

```python
import functools

import jax
import jax.numpy as jnp
from jax import lax
from jax.experimental import pallas as pl
from jax.experimental.pallas import tpu as pltpu

F32 = jnp.float32
BF16 = jnp.bfloat16

N_BRANCH = 4
SGU_GROUPS = 4
SGU_CHUNK = 128
LRU_HEADS = 4
CONV_W = 4
LRU_C = 8.0
POOL_WINDOWS = (2, 4, 8, 16)
POOL_BUF = max(POOL_WINDOWS) - 1
HG_HEADS = 4
X_HEADS = 4
N_EXPERTS = 16
N_GROUPS = 4
EXPERTS_PER_GROUP = N_EXPERTS // N_GROUPS
MODEL_DEPTH = 4
DN_ALPHA = (2 * MODEL_DEPTH) ** 0.25
LN_EPS = 1e-5
RMS_EPS = 1e-6
PAST_LEN = 16384
MOE_TILE = 1024
MOE_CHUNK = 160
SEG_ALIGN = 16

V7X_VMEM_BYTES = 64 * 1024 * 1024
VMEM_LIMIT = V7X_VMEM_BYTES - 8 * 1024 * 1024
SUBLANES = 8
LANES = 128


def _params(n_grid):
    return pltpu.CompilerParams(dimension_semantics=("arbitrary",) * n_grid, vmem_limit_bytes=VMEM_LIMIT)


def _resident(shape):
    nd = len(shape)
    return pl.BlockSpec(shape, lambda *_: (0,) * nd, pipeline_mode=pl.Buffered(1))


def _layer_norm(x, g, b):
    mu = jnp.mean(x, axis=-1, keepdims=True)
    xc = x - mu
    var = jnp.mean(xc * xc, axis=-1, keepdims=True)
    return xc * lax.rsqrt(var + LN_EPS) * g + b


def _sigmoid(x):
    return 1.0 / (1.0 + jnp.exp(-x))


def _bdot(a, b):
    return jnp.dot(a.astype(BF16), b.astype(BF16), preferred_element_type=F32)


def _bdot_nt(a, b):
    return lax.dot_general(a.astype(BF16), b.astype(BF16), (((1,), (1,)), ((), ())), preferred_element_type=F32)


def _bdot_tn(a, b):
    return lax.dot_general(a.astype(BF16), b.astype(BF16), (((0,), (0,)), ((), ())), preferred_element_type=F32)


def _shift_rows(x, k):
    return pltpu.roll(x, k, 0)


def _shift_rows_fill(x, k, fill):
    n, c = x.shape
    if k % SUBLANES == 0:
        return jnp.concatenate([jnp.full((k, c), fill, x.dtype), x[:n - k]], axis=0)
    row = lax.broadcasted_iota(jnp.int32, x.shape, 0)
    return jnp.where(row >= k, pltpu.roll(x, k, 0), fill)


def _rotate_in_groups(x, k):
    n, c = x.shape
    return pltpu.roll(x.reshape(n // SUBLANES, SUBLANES, c), k, 1).reshape(n, c)


def _ln_kernel(x_ref, g_ref, b_ref, o_ref):
    o_ref[...] = _layer_norm(x_ref[...], g_ref[...], b_ref[...])


def _input_ln(x, g, b, tile):
    n, d = x.shape
    return pl.pallas_call(
        _ln_kernel,
        grid=(n // tile,),
        in_specs=[pl.BlockSpec((tile, d), lambda i: (i, 0)), _resident((1, d)), _resident((1, d))],
        out_specs=pl.BlockSpec((tile, d), lambda i: (i, 0)),
        out_shape=jax.ShapeDtypeStruct((n, d), F32),
        compiler_params=_params(1),
        name="input_ln",
    )(x, g.reshape(1, d), b.reshape(1, d))


def _kv_kernel(m_ref, wk_ref, wv_ref, k_ref, v_ref, kb_ref, vb_ref, *, hd):
    m = m_ref[...].astype(BF16)
    k = jnp.dot(m, wk_ref[...], preferred_element_type=F32)
    v = jnp.dot(m, wv_ref[...], preferred_element_type=F32)
    for h in range(X_HEADS):
        k_ref[:, h, :] = k[:, h * hd:(h + 1) * hd]
        v_ref[:, h, :] = v[:, h * hd:(h + 1) * hd]
    kb_ref[...] = k.astype(BF16)
    vb_ref[...] = v.astype(BF16)


def _memory_kv(mem, wk, wv, tile):
    n, d = mem.shape
    depth = wk.shape[0]
    hd = d // X_HEADS
    out = jax.ShapeDtypeStruct((depth, n, X_HEADS, hd), F32)
    outb = jax.ShapeDtypeStruct((depth, n, d), BF16)
    return pl.pallas_call(
        functools.partial(_kv_kernel, hd=hd),
        grid=(depth, n // tile),
        in_specs=[
            pl.BlockSpec((tile, d), lambda l, i: (i, 0)),
            pl.BlockSpec((None, d, d), lambda l, i: (l, 0, 0)),
            pl.BlockSpec((None, d, d), lambda l, i: (l, 0, 0)),
        ],
        out_specs=[pl.BlockSpec((None, tile, X_HEADS, hd), lambda l, i: (l, i, 0, 0))] * 2
        + [pl.BlockSpec((None, tile, d), lambda l, i: (l, i, 0))] * 2,
        out_shape=[out, out, outb, outb],
        compiler_params=_params(2),
        name="memory_kv",
    )(mem, wk, wv)


def _seqmix_kernel(*refs, tile, sub, pos0, has_state, emit_v, br):
    (x_ref, win_ref, bin_ref, sgug_ref, sgub_ref, sguw_ref, sgubt_ref, convw_ref, convb_ref, wr_ref, br_ref,
     wi_ref, bi_ref, lam_ref, poolw_ref, poolb_ref, pools_ref, lb_ref, hgn_ref) = refs[:19]
    refs = refs[19:]
    if has_state:
        conv0_ref, h0_ref, pool0_ref, s0_ref = refs[:4]
        refs = refs[4:]
    obr_ref, convn_ref, hn_ref, pooln_ref, sn_ref = refs[:5]
    refs = refs[5:]
    if emit_v:
        vrows_ref = refs[0]
        refs = refs[1:]
    conv_sc, pool_sc, h_sc, st_sc = refs

    t = pl.program_id(1)
    n_t = pl.num_programs(1)
    gd = br // SGU_GROUPS
    hd = br // HG_HEADS

    @pl.when(t == 0)
    def _init():
        conv_sc[...] = jnp.zeros_like(conv_sc)
        pool_sc[...] = jnp.zeros_like(pool_sc)
        if has_state:
            conv_sc[SUBLANES - (CONV_W - 1):, :] = conv0_ref[...]
            pool_sc[1:, :] = pool0_ref[...]
            h_sc[...] = h0_ref[...]
            for h in range(HG_HEADS):
                st_sc[h] = s0_ref[h].T
        else:
            h_sc[...] = jnp.zeros_like(h_sc)
            st_sc[...] = jnp.zeros_like(st_sc)

    z = jnp.dot(x_ref[...].astype(BF16), win_ref[...], preferred_element_type=F32) + bin_ref[...]
    u, v, yg, xb, c, q, f, vi, go = [z[:, k * br:(k + 1) * br] for k in range(9)]
    row = lax.broadcasted_iota(jnp.int32, (tile, br), 0)

    vn = _layer_norm(v, sgug_ref[...], sgub_ref[...])
    if emit_v:
        vrows_ref[...] = vn
    rr = lax.broadcasted_iota(jnp.int32, (tile, tile), 0)
    cc = lax.broadcasted_iota(jnp.int32, (tile, tile), 1)
    o_a = []
    for g in range(SGU_GROUPS):
        w = jnp.where(rr >= cc, sguw_ref[g, :tile, :tile], 0.0)
        mixed = _bdot(w, vn[:, g * gd:(g + 1) * gd]) + sgubt_ref[:tile, g:g + 1]
        o_a.append(u[:, g * gd:(g + 1) * gd] * mixed)
    o_a = jnp.concatenate(o_a, axis=1)

    ext = jnp.concatenate([conv_sc[...], xb], axis=0)
    cw = convw_ref[...]
    base = SUBLANES - (CONV_W - 1)
    xc = convb_ref[...] + xb * cw[CONV_W - 1:CONV_W]
    for j in range(CONV_W - 1):
        xc = xc + ext[base + j:base + j + tile] * cw[j:j + 1]
    conv_sc[...] = ext[tile:tile + SUBLANES]
    convn_ref[...] = ext[tile + base:tile + SUBLANES]

    def heads_dot(a, w_ref):
        return jnp.concatenate([_bdot(a[:, h * hd:(h + 1) * hd], w_ref[h]) for h in range(LRU_HEADS)], axis=1)

    r_g = _sigmoid(heads_dot(xc, wr_ref) + br_ref[...])
    i_g = _sigmoid(heads_dot(xc, wi_ref) + bi_ref[...])
    lam = lam_ref[...]
    softplus_neg_lam = jnp.maximum(-lam, 0.0) + jnp.log1p(jnp.exp(-jnp.abs(lam)))
    log_a = (-LRU_C) * r_g * softplus_neg_lam
    a = jnp.exp(log_a)
    y2 = 2.0 * log_a
    one_minus = jnp.tanh(-0.5 * y2) * (jnp.exp(y2) + 1.0)
    bx = jnp.sqrt(one_minus) * (i_g * xc)
    sa, sb = a, bx
    k = 1
    while k < tile:
        sb = sa * _shift_rows_fill(sb, k, 0.0) + sb
        sa = sa * _shift_rows_fill(sa, k, 1.0)
        k *= 2
    hseq = sb + sa * h_sc[...]
    h_sc[...] = hseq[tile - 1:tile]
    hn_ref[...] = hseq[tile - 1:tile]
    gelu = 0.5 * yg * (1.0 + jnp.tanh(0.7978845608028654 * (yg + 0.044715 * (yg * yg * yg))))
    o_b = gelu * hseq

    extp = jnp.concatenate([pool_sc[...], c], axis=0)
    pb = POOL_BUF + 1
    sums = {1: extp}
    wdt = 1
    while wdt < max(POOL_WINDOWS):
        prev = sums[wdt]
        sums[2 * wdt] = prev + (_shift_rows_fill(prev, wdt, 0.0) if wdt % SUBLANES == 0 else _shift_rows(prev, wdt))
        wdt *= 2
    tpos = pos0 + 1 + t * tile + lax.broadcasted_iota(jnp.int32, (tile, 1), 0)
    o_c = []
    for g, wdw in enumerate(POOL_WINDOWS):
        cnt = jnp.minimum(wdw, tpos).astype(F32)
        pooled = sums[wdw][pb:pb + tile, g * gd:(g + 1) * gd] / cnt
        o_c.append(_bdot(pooled - c[:, g * gd:(g + 1) * gd], poolw_ref[g]))
    o_c = (jnp.concatenate(o_c, axis=1) + poolb_ref[...]) * pools_ref[...]
    pool_sc[...] = extp[tile:tile + pb]
    pooln_ref[...] = extp[tile + 1:tile + pb]

    lb = lb_ref[...]
    log_sig = jnp.minimum(f, 0.0) - jnp.log1p(jnp.exp(-jnp.abs(f)))
    a1 = jnp.log(lb)
    a2 = jnp.log1p(-lb) + log_sig
    log_f = jnp.maximum(a1, a2) + jnp.log1p(jnp.exp(-jnp.abs(a1 - a2)))
    kf = (1.0 - lb) * _sigmoid(-f)
    qf = q * _sigmoid(q)
    rmod = row % sub
    shift = _rotate_in_groups if sub == SUBLANES else _shift_rows
    bc = log_f
    k = 1
    while k < sub:
        bc = bc + jnp.where(rmod >= k, shift(bc, k), 0.0)
        k *= 2
    o_d = [jnp.zeros((tile, hd), F32) for _ in range(HG_HEADS)]
    for d in range(sub):
        if d == 0:
            p = qf * kf
            vd = vi
        else:
            diff = jnp.where(rmod >= d, bc - shift(bc, d), -jnp.inf)
            p = qf * shift(kf, d) * jnp.exp(diff)
            vd = shift(vi, d)
        for h in range(HG_HEADS):
            att = jnp.sum(p[:, h * hd:(h + 1) * hd], axis=-1, keepdims=True)
            o_d[h] = o_d[h] + att * vd[:, h * hd:(h + 1) * hd]
    qe = qf * jnp.exp(bc)
    for j in range(tile // sub):
        lo, hi = j * sub, (j + 1) * sub
        b_last = bc[hi - 1:hi]
        kdec = kf[lo:hi] * jnp.exp(b_last - bc[lo:hi])
        decay = jnp.exp(b_last)
        inter = []
        for h in range(HG_HEADS):
            st = st_sc[h]
            inter.append(_bdot_nt(qe[lo:hi, h * hd:(h + 1) * hd], st))
            upd = _bdot_tn(vi[lo:hi, h * hd:(h + 1) * hd], kdec[:, h * hd:(h + 1) * hd])
            st_sc[h] = st * decay[:, h * hd:(h + 1) * hd] + upd
        inter = jnp.concatenate(inter, axis=1)
        if j == 0:
            o_inter = [inter]
        else:
            o_inter.append(inter)
    o_inter = jnp.concatenate(o_inter, axis=0) if len(o_inter) > 1 else o_inter[0]
    gn = hgn_ref[...]
    o_n = []
    for h in range(HG_HEADS):
        oh = o_d[h] + o_inter[:, h * hd:(h + 1) * hd]
        ms = jnp.mean(oh * oh, axis=-1, keepdims=True)
        o_n.append(oh * lax.rsqrt(ms + RMS_EPS) * gn)
    o_d = jnp.concatenate(o_n, axis=1) * (go * _sigmoid(go))

    obr_ref[...] = jnp.concatenate([o_a, o_b, o_c, o_d], axis=1).astype(BF16)

    @pl.when(t == n_t - 1)
    def _emit_state():
        for h in range(HG_HEADS):
            sn_ref[h] = st_sc[h].T


def _seqmix(x, p, lb, states, *, batch, seq, tile, sub, pos0, emit_v):
    n, d = x.shape
    br = d // 2
    hd = br // HG_HEADS
    n_t = seq // tile
    has_state = states is not None
    mixw = 9 * br
    row = lambda a: a.reshape(1, -1)
    ins = [x, p["w_in_mix"], row(p["b_in"][:mixw]), row(p["sgu_ln_g"]), row(p["sgu_ln_b"]), p["sgu_w"],
           p["sgu_b"].T, p["conv_w"], row(p["conv_b"]), p["lru_wr"], row(p["lru_br"]), p["lru_wi"], row(p["lru_bi"]),
           row(p["lru_lambda"]), p["pool_w"], row(p["pool_b"]), row(p["pool_scale"]), row(lb), row(p["hgrn_norm_g"])]
    in_specs = [pl.BlockSpec((tile, d), lambda b, t: (b * n_t + t, 0))] + [_resident(a.shape) for a in ins[1:]]
    if has_state:
        conv0, h0, pool0, s0 = states
        ins += [conv0, h0.reshape(batch, 1, br), pool0, s0]
        in_specs += [
            pl.BlockSpec((None, CONV_W - 1, br), lambda b, t: (b, 0, 0)),
            pl.BlockSpec((None, 1, br), lambda b, t: (b, 0, 0)),
            pl.BlockSpec((None, POOL_BUF, br), lambda b, t: (b, 0, 0)),
            pl.BlockSpec((None, HG_HEADS, hd, hd), lambda b, t: (b, 0, 0, 0)),
        ]
    out_shape = [
        jax.ShapeDtypeStruct((n, N_BRANCH * br), BF16),
        jax.ShapeDtypeStruct((batch, CONV_W - 1, br), F32),
        jax.ShapeDtypeStruct((batch, 1, br), F32),
        jax.ShapeDtypeStruct((batch, POOL_BUF, br), F32),
        jax.ShapeDtypeStruct((batch, HG_HEADS, hd, hd), F32),
    ]
    out_specs = [
        pl.BlockSpec((tile, N_BRANCH * br), lambda b, t: (b * n_t + t, 0)),
        pl.BlockSpec((None, CONV_W - 1, br), lambda b, t: (b, 0, 0)),
        pl.BlockSpec((None, 1, br), lambda b, t: (b, 0, 0)),
        pl.BlockSpec((None, POOL_BUF, br), lambda b, t: (b, 0, 0)),
        pl.BlockSpec((None, HG_HEADS, hd, hd), lambda b, t: (b, 0, 0, 0)),
    ]
    if emit_v:
        out_shape.append(jax.ShapeDtypeStruct((n, br), F32))
        out_specs.append(pl.BlockSpec((tile, br), lambda b, t: (b * n_t + t, 0)))
    scratch = [
        pltpu.VMEM((SUBLANES, br), F32),
        pltpu.VMEM((POOL_BUF + 1, br), F32),
        pltpu.VMEM((1, br), F32),
        pltpu.VMEM((HG_HEADS, hd, hd), F32),
    ]
    kern = functools.partial(_seqmix_kernel, tile=tile, sub=sub, pos0=pos0, has_state=has_state, emit_v=emit_v, br=br)
    outs = pl.pallas_call(
        kern,
        grid=(batch, n_t),
        in_specs=in_specs,
        out_specs=out_specs,
        out_shape=out_shape,
        scratch_shapes=scratch,
        compiler_params=_params(2),
        name="seqmix_state" if has_state else "seqmix",
    )(*ins)
    return outs


def _merge_kernel(x_ref, obr_ref, wg_ref, bg_ref, wb_ref, wo_ref, lng_ref, lnb_ref, wq_ref, x1_ref, q_ref, *, br, d):
    x = x_ref[...]
    xb16 = x.astype(BF16)
    acc = None
    for g in range(N_BRANCH):
        gz = jnp.dot(xb16, wg_ref[:, g * d:(g + 1) * d], preferred_element_type=F32) + bg_ref[:, g * d:(g + 1) * d]
        proj = jnp.dot(obr_ref[:, g * br:(g + 1) * br], wb_ref[g], preferred_element_type=F32)
        term = _sigmoid(gz) * proj
        acc = term if acc is None else acc + term
    y = _bdot(acc, wo_ref[...])
    x1 = _layer_norm(DN_ALPHA * x + y, lng_ref[...], lnb_ref[...])
    x1_ref[...] = x1
    q_ref[...] = _bdot(x1, wq_ref[...]).astype(BF16)


def _merge(x, obr, p, tile):
    n, d = x.shape
    br = d // 2
    ins = [x, obr, p["w_in_gate"], p["b_in"][9 * br:].reshape(1, -1), p["w_branch"], p["w_out"],
           p["ln_g"][0].reshape(1, d), p["ln_b"][0].reshape(1, d), p["x_wq"]]
    in_specs = [pl.BlockSpec((tile, d), lambda i: (i, 0)), pl.BlockSpec((tile, N_BRANCH * br), lambda i: (i, 0))]
    in_specs += [_resident(a.shape) for a in ins[2:]]
    return pl.pallas_call(
        functools.partial(_merge_kernel, br=br, d=d),
        grid=(n // tile,),
        in_specs=in_specs,
        out_specs=[pl.BlockSpec((tile, d), lambda i: (i, 0))] * 2,
        out_shape=[jax.ShapeDtypeStruct((n, d), F32), jax.ShapeDtypeStruct((n, d), BF16)],
        compiler_params=_params(1),
        name="merge",
    )(*ins)


def _attn_kernel(q_ref, k_ref, v_ref, o_ref, *, hd):
    scale = hd ** -0.5
    outs = []
    for h in range(X_HEADS):
        sl = slice(h * hd, (h + 1) * hd)
        s = _bdot_nt(q_ref[:, sl], k_ref[:, sl]) * scale
        e = jnp.exp(s - jnp.max(s, axis=-1, keepdims=True))
        prob = e / jnp.sum(e, axis=-1, keepdims=True)
        outs.append(_bdot(prob, v_ref[:, sl]))
    o_ref[...] = jnp.concatenate(outs, axis=1).astype(BF16)


def _attn(q, mem_k, mem_v, layer, *, batch, seq, tile):
    n, d = q.shape
    n_t = seq // tile
    m = mem_k.shape[2]
    kv_spec = pl.BlockSpec((None, None, m, d), lambda b, t: (layer, b, 0, 0))
    return pl.pallas_call(
        functools.partial(_attn_kernel, hd=d // X_HEADS),
        grid=(batch, n_t),
        in_specs=[pl.BlockSpec((tile, d), lambda b, t: (b * n_t + t, 0)), kv_spec, kv_spec],
        out_specs=pl.BlockSpec((tile, d), lambda b, t: (b * n_t + t, 0)),
        out_shape=jax.ShapeDtypeStruct((n, d), BF16),
        compiler_params=_params(2),
        name="attn",
    )(q, mem_k, mem_v)


def _route(s, sel):
    neg = -jnp.inf

    def top2(vals):
        m1 = functools.reduce(jnp.maximum, vals)
        i1 = jnp.full(m1.shape, len(vals) - 1, jnp.int32)
        for j in range(len(vals) - 2, -1, -1):
            i1 = jnp.where(vals[j] == m1, j, i1)
        rest = [jnp.where(i1 == j, neg, vals[j]) for j in range(len(vals))]
        m2 = functools.reduce(jnp.maximum, rest)
        i2 = jnp.full(m1.shape, len(vals) - 1, jnp.int32)
        for j in range(len(vals) - 2, -1, -1):
            i2 = jnp.where(rest[j] == m2, j, i2)
        return m1, m2, i1, i2

    groups = [top2(sel[g * EXPERTS_PER_GROUP:(g + 1) * EXPERTS_PER_GROUP]) for g in range(N_GROUPS)]
    score = [m1 + m2 for (m1, m2, _, _) in groups]
    best = functools.reduce(jnp.maximum, score)
    g_idx = jnp.full(best.shape, N_GROUPS - 1, jnp.int32)
    for g in range(N_GROUPS - 2, -1, -1):
        g_idx = jnp.where(score[g] == best, g, g_idx)
    e1 = jnp.zeros(best.shape, jnp.int32)
    e2 = jnp.zeros(best.shape, jnp.int32)
    for g in range(N_GROUPS):
        e1 = jnp.where(g_idx == g, g * EXPERTS_PER_GROUP + groups[g][2], e1)
        e2 = jnp.where(g_idx == g, g * EXPERTS_PER_GROUP + groups[g][3], e2)
    w1 = functools.reduce(jnp.add, [jnp.where(e1 == e, s[e], 0.0) for e in range(N_EXPERTS)])
    w2 = functools.reduce(jnp.add, [jnp.where(e2 == e, s[e], 0.0) for e in range(N_EXPERTS)])
    tot = w1 + w2
    return e1, e2, w1 / tot, w2 / tot


def _post_kernel(x1_ref, o_ref, wo_ref, lng_ref, lnb_ref, rwt_ref, rb_ref, tri_ref,
                 x2_ref, x2b_ref, rows_ref, cols_ref, seg_ref, *, tile):
    y = jnp.dot(o_ref[...], wo_ref[...], preferred_element_type=F32)
    x2 = _layer_norm(DN_ALPHA * x1_ref[...] + y, lng_ref[...], lnb_ref[...])
    x2_ref[...] = x2
    x2b_ref[...] = x2.astype(BF16)
    logits = lax.dot_general(rwt_ref[...], x2, (((1,), (1,)), ((), ())), preferred_element_type=F32,
                             precision=lax.Precision.HIGHEST)
    s_all = _sigmoid(logits)
    sel_all = s_all + rb_ref[...]
    s = [s_all[e:e + 1] for e in range(N_EXPERTS)]
    sel = [sel_all[e:e + 1] for e in range(N_EXPERTS)]
    e1, e2, w1, w2 = _route(s, sel)

    onehot = jnp.concatenate([jnp.where((e1 == e) | (e2 == e), 1.0, 0.0) for e in range(N_EXPERTS)], axis=0)
    prefix = jnp.dot(onehot.astype(BF16), tri_ref[...], preferred_element_type=F32)
    cnt = jnp.broadcast_to(jnp.sum(onehot, axis=-1, keepdims=True), (N_EXPERTS, LANES))
    cpad = jnp.floor((cnt + (SEG_ALIGN - 1.0)) * (1.0 / SEG_ALIGN)) * SEG_ALIGN
    ridx = lax.broadcasted_iota(jnp.int32, (N_EXPERTS, LANES), 0)
    inc = cpad
    k = 1
    while k < N_EXPERTS:
        inc = inc + jnp.where(ridx >= k, _shift_rows(inc, k), 0.0)
        k *= 2
    start = inc - cpad

    def pick(e_row, table):
        return functools.reduce(jnp.add, [jnp.where(e_row == e, table[e:e + 1], 0.0) for e in range(N_EXPERTS)])

    start_col = start[:, 0:1]
    pos1 = pick(e1, start_col) + pick(e1, prefix)
    pos2 = pick(e2, start_col) + pick(e2, prefix)
    rows = jnp.concatenate([pos1, pos2, w1, w2, jnp.zeros((SUBLANES - 4, tile), F32)], axis=0)
    rows_ref[...] = rows
    padded = jnp.concatenate([rows, jnp.zeros((LANES - SUBLANES, tile), F32)], axis=0)
    for j in range(tile // LANES):
        cols_ref[j * LANES:(j + 1) * LANES, :] = padded[:, j * LANES:(j + 1) * LANES].T
    seg_ref[0] = start
    seg_ref[1] = cnt


def _post(x1, o, p, router_wt, router_b, tri, tile):
    n, d = x1.shape
    ins = [x1, o, p["x_wo"], p["ln_g"][1].reshape(1, d), p["ln_b"][1].reshape(1, d), router_wt,
           router_b.reshape(N_EXPERTS, 1), tri]
    in_specs = [pl.BlockSpec((tile, d), lambda i: (i, 0))] * 2 + [_resident(a.shape) for a in ins[2:]]
    return pl.pallas_call(
        functools.partial(_post_kernel, tile=tile),
        grid=(n // tile,),
        in_specs=in_specs,
        out_specs=[pl.BlockSpec((tile, d), lambda i: (i, 0))] * 2 + [
            pl.BlockSpec((SUBLANES, tile), lambda i: (0, i)),
            pl.BlockSpec((tile, LANES), lambda i: (i, 0)),
            pl.BlockSpec((None, 2, N_EXPERTS, LANES), lambda i: (i, 0, 0, 0))],
        out_shape=[jax.ShapeDtypeStruct((n, d), F32), jax.ShapeDtypeStruct((n, d), BF16),
                   jax.ShapeDtypeStruct((SUBLANES, n), F32), jax.ShapeDtypeStruct((n, LANES), F32),
                   jax.ShapeDtypeStruct((n // tile, 2, N_EXPERTS, LANES), F32)],
        compiler_params=_params(1),
        name="post_attn_router",
    )(*ins)


def _moe_kernel(start_ref, cnt_ref, x2_ref, xb_ref, rows_ref, cols_ref, wg_ref, wu_ref, wd_ref, lng_ref, lnb_ref,
                out_ref, xs_ref, ys_ref, ws_ref, *, tile, rcap, rblk, tblk):
    i = pl.program_id(0)
    e = pl.program_id(1)

    @pl.when(e == 0)
    def _sort_rows():
        pos1, pos2 = rows_ref[0:1, :], rows_ref[1:2, :]
        w1, w2 = rows_ref[2:3, :], rows_ref[3:4, :]
        xb = xb_ref[...]
        for c in range(rcap // rblk):
            r = (c * rblk + lax.broadcasted_iota(jnp.int32, (rblk, tile), 0)).astype(F32)
            m1 = pos1 == r
            m2 = pos2 == r
            sel = jnp.where(m1 | m2, 1.0, 0.0).astype(BF16)
            xs_ref[c * rblk:(c + 1) * rblk, :] = jnp.dot(sel, xb, preferred_element_type=F32).astype(BF16)
            ws_ref[c * rblk:(c + 1) * rblk, :] = jnp.sum(jnp.where(m1, w1, 0.0) + jnp.where(m2, w2, 0.0),
                                                       axis=-1, keepdims=True)
        ys_ref[...] = jnp.zeros_like(ys_ref)

    start = start_ref[i * N_EXPERTS + e]
    n_chunks = cnt_ref[i * N_EXPERTS + e]

    def chunk_body(j, carry):
        r = pl.multiple_of(start + j * MOE_CHUNK, SEG_ALIGN)
        xs = xs_ref[pl.ds(r, MOE_CHUNK), :]
        gate = jnp.dot(xs, wg_ref[...], preferred_element_type=F32)
        up = jnp.dot(xs, wu_ref[...], preferred_element_type=F32)
        hid = gate * _sigmoid(gate) * up
        y = _bdot(hid, wd_ref[...])
        ys_ref[pl.ds(r, MOE_CHUNK), :] = (ws_ref[pl.ds(r, MOE_CHUNK), :] * y).astype(BF16)
        return carry

    lax.fori_loop(0, n_chunks, chunk_body, 0)

    @pl.when(e == pl.num_programs(1) - 1)
    def _finish():
        for c in range(tile // tblk):
            sl = slice(c * tblk, (c + 1) * tblk)
            r = lax.broadcasted_iota(jnp.int32, (tblk, rcap), 1).astype(F32)
            back = jnp.where((cols_ref[sl, 0:1] == r) | (cols_ref[sl, 1:2] == r), 1.0, 0.0).astype(BF16)
            y = jnp.dot(back, ys_ref[...], preferred_element_type=F32)
            out_ref[sl, :] = _layer_norm(DN_ALPHA * x2_ref[sl, :] + y, lng_ref[...], lnb_ref[...])


def _moe(x2, x2b, rows, cols, seg, p, tile):
    n, d = x2.shape
    de = p["e_wg"].shape[-1]
    tblk = min(256, tile)
    rcap = -(-(2 * tile + N_EXPERTS * SEG_ALIGN + MOE_CHUNK) // SEG_ALIGN) * SEG_ALIGN
    rblk = max(b for b in range(SEG_ALIGN, 257, SEG_ALIGN) if rcap % b == 0)
    seg_start = seg[:, 0, :, 0].astype(jnp.int32).reshape(-1)
    seg_cnt = (seg[:, 1, :, 0].astype(jnp.int32).reshape(-1) + (MOE_CHUNK - 1)) // MOE_CHUNK
    row_spec = lambda w: pl.BlockSpec((tile, w), lambda i, e, s, c: (i, 0))
    grid_spec = pltpu.PrefetchScalarGridSpec(
        num_scalar_prefetch=2,
        grid=(n // tile, N_EXPERTS),
        in_specs=[row_spec(d), row_spec(d),
                  pl.BlockSpec((SUBLANES, tile), lambda i, e, s, c: (0, i)),
                  row_spec(LANES),
                  pl.BlockSpec((None, d, de), lambda i, e, s, c: (e, 0, 0)),
                  pl.BlockSpec((None, d, de), lambda i, e, s, c: (e, 0, 0)),
                  pl.BlockSpec((None, de, d), lambda i, e, s, c: (e, 0, 0)),
                  pl.BlockSpec((1, d), lambda i, e, s, c: (0, 0)),
                  pl.BlockSpec((1, d), lambda i, e, s, c: (0, 0))],
        out_specs=row_spec(d),
        scratch_shapes=[pltpu.VMEM((rcap, d), BF16), pltpu.VMEM((rcap, d), BF16), pltpu.VMEM((rcap, 1), F32)],
    )
    return pl.pallas_call(
        functools.partial(_moe_kernel, tile=tile, rcap=rcap, rblk=rblk, tblk=tblk),
        grid_spec=grid_spec,
        out_shape=jax.ShapeDtypeStruct((n, d), F32),
        compiler_params=_params(2),
        name="moe",
    )(seg_start, seg_cnt, x2, x2b, rows, cols, p["e_wg"], p["e_wu"], p["e_wd"],
      p["ln_g"][2].reshape(1, d), p["ln_b"][2].reshape(1, d))


def _pick(n, pref):
    return pref if n % pref == 0 else n


def kernel(x_prompt, x_sample, mem_prompt, cache_mem_k, cache_mem_v, state_conv, state_rglru, state_pool, state_hgrn, ln_in_g, ln_in_b, w_in, b_in, sgu_ln_g, sgu_ln_b, sgu_w, sgu_b, conv_w, conv_b, lru_wr, lru_br, lru_wi, lru_bi, lru_lambda, pool_w, pool_b, pool_scale, hgrn_lb, hgrn_norm_g, w_branch, w_out, x_wq, x_wk, x_wv, x_wo, ln_g, ln_b, router_w, router_b, e_wg, e_wu, e_wd):
    bp, tp, d = x_prompt.shape
    bs, ts, _ = x_sample.shape
    depth = w_in.shape[0]
    br = d // 2
    n_mem = mem_prompt.shape[1]
    hd = br // HG_HEADS
    mixw = 9 * br
    assert depth == MODEL_DEPTH

    lb_cum = jnp.cumsum(jax.nn.softmax(hgrn_lb.astype(F32), axis=0), axis=0)
    lbs = lb_cum - lb_cum[0]

    w_in_mix = w_in[:, :, :mixw].astype(BF16)
    w_in_gate = w_in[:, :, mixw:].astype(BF16)
    wts = dict(w_branch=w_branch, w_out=w_out, x_wq=x_wq, x_wo=x_wo, lru_wr=lru_wr, lru_wi=lru_wi, pool_w=pool_w,
               e_wg=e_wg, e_wu=e_wu, e_wd=e_wd)
    wts = {k: v.astype(BF16) for k, v in wts.items()}
    router_wt = router_w.T

    tile_p = SGU_CHUNK if tp % SGU_CHUNK == 0 else tp
    tile_s = SGU_CHUNK if ts % SGU_CHUNK == 0 else ts
    sub_p = _pick(tile_p, SUBLANES)
    sub_s = _pick(tile_s, SUBLANES)

    n_p, n_s = bp * tp, bs * ts
    xp = _input_ln(x_prompt.reshape(n_p, d), ln_in_g, ln_in_b, _pick(n_p, 512))
    xs = _input_ln(x_sample.reshape(n_s, d), ln_in_g, ln_in_b, _pick(n_s, 512))

    mem_flat = mem_prompt.reshape(bp * n_mem, d)
    mk_all, mv_all, mkb, mvb = _memory_kv(mem_flat, x_wk.astype(BF16), x_wv.astype(BF16), _pick(bp * n_mem, 512))
    mk_all = mk_all.reshape(depth, bp, n_mem, X_HEADS, d // X_HEADS)
    mv_all = mv_all.reshape(depth, bp, n_mem, X_HEADS, d // X_HEADS)
    mkb = mkb.reshape(depth, bp, n_mem, d)
    mvb = mvb.reshape(depth, bp, n_mem, d)
    ck = cache_mem_k.astype(BF16).reshape(depth, bs, n_mem, d)
    cv = cache_mem_v.astype(BF16).reshape(depth, bs, n_mem, d)

    moe_tile_p = _pick(n_p, MOE_TILE)
    moe_tile_s = _pick(n_s, MOE_TILE)

    def strict_upper(t):
        idx = jnp.arange(t)
        return (idx[:, None] < idx[None, :]).astype(BF16)

    tri_p, tri_s = strict_upper(moe_tile_p), strict_upper(moe_tile_s)

    outs_p, outs_s = [], []
    for l in range(depth):
        p = dict(w_in_mix=w_in_mix[l], w_in_gate=w_in_gate[l], b_in=b_in[l], sgu_ln_g=sgu_ln_g[l],
                 sgu_ln_b=sgu_ln_b[l], sgu_w=sgu_w[l], sgu_b=sgu_b[l], conv_w=conv_w[l], conv_b=conv_b[l],
                 lru_br=lru_br[l], lru_bi=lru_bi[l], lru_lambda=lru_lambda[l], pool_b=pool_b[l],
                 pool_scale=pool_scale[l], hgrn_norm_g=hgrn_norm_g[l], ln_g=ln_g[l], ln_b=ln_b[l])
        p.update({k: v[l] for k, v in wts.items()})

        obr_p, conv_p, h_p, pool_p, s_p = _seqmix(xp, p, lbs[l], None, batch=bp, seq=tp, tile=tile_p, sub=sub_p,
                                                   pos0=0, emit_v=False)
        obr_s, conv_s, h_s, pool_s, s_s, v_s = _seqmix(
            xs, p, lbs[l], (state_conv[l], state_rglru[l], state_pool[l], state_hgrn[l]),
            batch=bs, seq=ts, tile=tile_s, sub=sub_s, pos0=PAST_LEN, emit_v=True)
        outs_p.append((conv_p, h_p.reshape(bp, br), pool_p, s_p))
        outs_s.append((conv_s, h_s.reshape(bs, br), pool_s, s_s, v_s.reshape(bs, ts, br)))

        x1p, qp = _merge(xp, obr_p, p, _pick(n_p, 256))
        x1s, qs = _merge(xs, obr_s, p, _pick(n_s, 256))
        op = _attn(qp, mkb, mvb, l, batch=bp, seq=tp, tile=_pick(tp, 256))
        os_ = _attn(qs, ck, cv, l, batch=bs, seq=ts, tile=ts)
        x2p, x2pb, rows_p, cols_p, seg_p = _post(x1p, op, p, router_wt, router_b, tri_p, moe_tile_p)
        x2s, x2sb, rows_s, cols_s, seg_s = _post(x1s, os_, p, router_wt, router_b, tri_s, moe_tile_s)
        xp = _moe(x2p, x2pb, rows_p, cols_p, seg_p, p, moe_tile_p)
        xs = _moe(x2s, x2sb, rows_s, cols_s, seg_s, p, moe_tile_s)

    stack = lambda outs, i: jnp.stack([o[i] for o in outs])
    return (xp.reshape(bp, tp, d), xs.reshape(bs, ts, d), mk_all, mv_all,
            stack(outs_p, 0), stack(outs_p, 1), stack(outs_p, 2), stack(outs_p, 3),
            stack(outs_s, 0), stack(outs_s, 1), stack(outs_s, 2), stack(outs_s, 3), stack(outs_s, 4))
```

```python
import functools

import jax
import jax.numpy as jnp
from jax import lax
from jax.experimental import pallas as pl
from jax.experimental.pallas import tpu as pltpu

F32 = jnp.float32
BF16 = jnp.bfloat16

N_BRANCH = 4
SGU_GROUPS = 4
SGU_CHUNK = 128
LRU_HEADS = 4
CONV_W = 4
LRU_C = 8.0
POOL_WINDOWS = (2, 4, 8, 16)
POOL_BUF = max(POOL_WINDOWS) - 1
HG_HEADS = 4
X_HEADS = 4
N_EXPERTS = 16
N_GROUPS = 4
EXPERTS_PER_GROUP = N_EXPERTS // N_GROUPS
MODEL_DEPTH = 4
DN_ALPHA = (2 * MODEL_DEPTH) ** 0.25
LN_EPS = 1e-5
RMS_EPS = 1e-6
PAST_LEN = 16384
MOE_TILE = 1024
MOE_CHUNK = 160
SEG_ALIGN = 16
CACHE_ROWS = 4

V7X_VMEM_BYTES = 64 * 1024 * 1024
VMEM_LIMIT = V7X_VMEM_BYTES - 8 * 1024 * 1024
SUBLANES = 8
LANES = 128


def _params(n_grid):
    return pltpu.CompilerParams(dimension_semantics=("arbitrary",) * n_grid, vmem_limit_bytes=VMEM_LIMIT)


def _resident(shape):
    nd = len(shape)
    return pl.BlockSpec(shape, lambda *_: (0,) * nd, pipeline_mode=pl.Buffered(1))


def _layer_norm(x, g, b):
    mu = jnp.mean(x, axis=-1, keepdims=True)
    xc = x - mu
    var = jnp.mean(xc * xc, axis=-1, keepdims=True)
    return xc * lax.rsqrt(var + LN_EPS) * g + b


def _sigmoid(x):
    return 1.0 / (1.0 + jnp.exp(-x))


def _bdot(a, b):
    return jnp.dot(a.astype(BF16), b.astype(BF16), preferred_element_type=F32)


def _bdot_nt(a, b):
    return lax.dot_general(a.astype(BF16), b.astype(BF16), (((1,), (1,)), ((), ())), preferred_element_type=F32)


def _bdot_tn(a, b):
    return lax.dot_general(a.astype(BF16), b.astype(BF16), (((0,), (0,)), ((), ())), preferred_element_type=F32)


def _shift_rows(x, k):
    return pltpu.roll(x, k, 0)


def _shift_rows_fill(x, k, fill):
    n, c = x.shape
    if k % SUBLANES == 0:
        return jnp.concatenate([jnp.full((k, c), fill, x.dtype), x[:n - k]], axis=0)
    row = lax.broadcasted_iota(jnp.int32, x.shape, 0)
    return jnp.where(row >= k, pltpu.roll(x, k, 0), fill)


def _rotate_in_groups(x, k):
    n, c = x.shape
    return pltpu.roll(x.reshape(n // SUBLANES, SUBLANES, c), k, 1).reshape(n, c)


def _ln_kernel(x_ref, g_ref, b_ref, o_ref):
    o_ref[...] = _layer_norm(x_ref[...], g_ref[...], b_ref[...])


def _input_ln(x, g, b, tile):
    n, d = x.shape
    return pl.pallas_call(
        _ln_kernel,
        grid=(n // tile,),
        in_specs=[pl.BlockSpec((tile, d), lambda i: (i, 0)), _resident((1, d)), _resident((1, d))],
        out_specs=pl.BlockSpec((tile, d), lambda i: (i, 0)),
        out_shape=jax.ShapeDtypeStruct((n, d), F32),
        compiler_params=_params(1),
        name="input_ln",
    )(x, g.reshape(1, d), b.reshape(1, d))


def _kv_kernel(m_ref, wk_ref, wv_ref, k_ref, v_ref, kb_ref, vb_ref, *, hd):
    m = m_ref[...].astype(BF16)
    k = jnp.dot(m, wk_ref[...], preferred_element_type=F32)
    v = jnp.dot(m, wv_ref[...], preferred_element_type=F32)
    for h in range(X_HEADS):
        k_ref[:, h, :] = k[:, h * hd:(h + 1) * hd]
        v_ref[:, h, :] = v[:, h * hd:(h + 1) * hd]
    kb_ref[...] = k.astype(BF16)
    vb_ref[...] = v.astype(BF16)


def _memory_kv(mem, wk, wv, tile):
    n, d = mem.shape
    depth = wk.shape[0]
    hd = d // X_HEADS
    out = jax.ShapeDtypeStruct((depth, n, X_HEADS, hd), F32)
    outb = jax.ShapeDtypeStruct((depth, n, d), BF16)
    return pl.pallas_call(
        functools.partial(_kv_kernel, hd=hd),
        grid=(depth, n // tile),
        in_specs=[
            pl.BlockSpec((tile, d), lambda l, i: (i, 0)),
            pl.BlockSpec((None, d, d), lambda l, i: (l, 0, 0)),
            pl.BlockSpec((None, d, d), lambda l, i: (l, 0, 0)),
        ],
        out_specs=[pl.BlockSpec((None, tile, X_HEADS, hd), lambda l, i: (l, i, 0, 0))] * 2
        + [pl.BlockSpec((None, tile, d), lambda l, i: (l, i, 0))] * 2,
        out_shape=[out, out, outb, outb],
        compiler_params=_params(2),
        name="memory_kv",
    )(mem, wk, wv)


def _seqmix_kernel(*refs, tile, sub, pos0, has_state, emit_v, br):
    (x_ref, win_ref, bin_ref, sgug_ref, sgub_ref, sguw_ref, sgubt_ref, convw_ref, convb_ref, wr_ref, br_ref,
     wi_ref, bi_ref, lam_ref, poolw_ref, poolb_ref, pools_ref, lb_ref, hgn_ref) = refs[:19]
    refs = refs[19:]
    if has_state:
        conv0_ref, h0_ref, pool0_ref, s0_ref = refs[:4]
        refs = refs[4:]
    obr_ref, convn_ref, hn_ref, pooln_ref, sn_ref = refs[:5]
    refs = refs[5:]
    if emit_v:
        vrows_ref = refs[0]
        refs = refs[1:]
    conv_sc, pool_sc, h_sc, st_sc = refs

    t = pl.program_id(1)
    n_t = pl.num_programs(1)
    gd = br // SGU_GROUPS
    hd = br // HG_HEADS

    @pl.when(t == 0)
    def _init():
        conv_sc[...] = jnp.zeros_like(conv_sc)
        pool_sc[...] = jnp.zeros_like(pool_sc)
        if has_state:
            conv_sc[SUBLANES - (CONV_W - 1):, :] = conv0_ref[...]
            pool_sc[1:, :] = pool0_ref[...]
            h_sc[...] = h0_ref[...]
            for h in range(HG_HEADS):
                st_sc[h] = s0_ref[h].T
        else:
            h_sc[...] = jnp.zeros_like(h_sc)
            st_sc[...] = jnp.zeros_like(st_sc)

    z = jnp.dot(x_ref[...].astype(BF16), win_ref[...], preferred_element_type=F32) + bin_ref[...]
    u, v, yg, xb, c, q, f, vi, go = [z[:, k * br:(k + 1) * br] for k in range(9)]
    row = lax.broadcasted_iota(jnp.int32, (tile, br), 0)

    vn = _layer_norm(v, sgug_ref[...], sgub_ref[...])
    if emit_v:
        vrows_ref[...] = vn
    rr = lax.broadcasted_iota(jnp.int32, (tile, tile), 0)
    cc = lax.broadcasted_iota(jnp.int32, (tile, tile), 1)
    o_a = []
    for g in range(SGU_GROUPS):
        w = jnp.where(rr >= cc, sguw_ref[g, :tile, :tile], 0.0)
        mixed = _bdot(w, vn[:, g * gd:(g + 1) * gd]) + sgubt_ref[:tile, g:g + 1]
        o_a.append(u[:, g * gd:(g + 1) * gd] * mixed)
    o_a = jnp.concatenate(o_a, axis=1)

    ext = jnp.concatenate([conv_sc[...], xb], axis=0)
    cw = convw_ref[...]
    base = SUBLANES - (CONV_W - 1)
    xc = convb_ref[...] + xb * cw[CONV_W - 1:CONV_W]
    for j in range(CONV_W - 1):
        xc = xc + ext[base + j:base + j + tile] * cw[j:j + 1]
    conv_sc[...] = ext[tile:tile + SUBLANES]
    convn_ref[...] = ext[tile + base:tile + SUBLANES]

    def heads_dot(a, w_ref):
        return jnp.concatenate([_bdot(a[:, h * hd:(h + 1) * hd], w_ref[h]) for h in range(LRU_HEADS)], axis=1)

    r_g = _sigmoid(heads_dot(xc, wr_ref) + br_ref[...])
    i_g = _sigmoid(heads_dot(xc, wi_ref) + bi_ref[...])
    lam = lam_ref[...]
    softplus_neg_lam = jnp.maximum(-lam, 0.0) + jnp.log1p(jnp.exp(-jnp.abs(lam)))
    log_a = (-LRU_C) * r_g * softplus_neg_lam
    a = jnp.exp(log_a)
    y2 = 2.0 * log_a
    one_minus = jnp.tanh(-0.5 * y2) * (jnp.exp(y2) + 1.0)
    bx = jnp.sqrt(one_minus) * (i_g * xc)
    sa, sb = a, bx
    k = 1
    while k < tile:
        sb = sa * _shift_rows_fill(sb, k, 0.0) + sb
        sa = sa * _shift_rows_fill(sa, k, 1.0)
        k *= 2
    hseq = sb + sa * h_sc[...]
    h_sc[...] = hseq[tile - 1:tile]
    hn_ref[...] = hseq[tile - 1:tile]
    gelu = 0.5 * yg * (1.0 + jnp.tanh(0.7978845608028654 * (yg + 0.044715 * (yg * yg * yg))))
    o_b = gelu * hseq

    extp = jnp.concatenate([pool_sc[...], c], axis=0)
    pb = POOL_BUF + 1
    sums = {1: extp}
    wdt = 1
    while wdt < max(POOL_WINDOWS):
        prev = sums[wdt]
        sums[2 * wdt] = prev + (_shift_rows_fill(prev, wdt, 0.0) if wdt % SUBLANES == 0 else _shift_rows(prev, wdt))
        wdt *= 2
    tpos = pos0 + 1 + t * tile + lax.broadcasted_iota(jnp.int32, (tile, 1), 0)
    o_c = []
    for g, wdw in enumerate(POOL_WINDOWS):
        cnt = jnp.minimum(wdw, tpos).astype(F32)
        pooled = sums[wdw][pb:pb + tile, g * gd:(g + 1) * gd] / cnt
        o_c.append(_bdot(pooled - c[:, g * gd:(g + 1) * gd], poolw_ref[g]))
    o_c = (jnp.concatenate(o_c, axis=1) + poolb_ref[...]) * pools_ref[...]
    pool_sc[...] = extp[tile:tile + pb]
    pooln_ref[...] = extp[tile + 1:tile + pb]

    lb = lb_ref[...]
    log_sig = jnp.minimum(f, 0.0) - jnp.log1p(jnp.exp(-jnp.abs(f)))
    a1 = jnp.log(lb)
    a2 = jnp.log1p(-lb) + log_sig
    log_f = jnp.maximum(a1, a2) + jnp.log1p(jnp.exp(-jnp.abs(a1 - a2)))
    kf = (1.0 - lb) * _sigmoid(-f)
    qf = q * _sigmoid(q)
    rmod = row % sub
    shift = _rotate_in_groups if sub == SUBLANES else _shift_rows
    bc = log_f
    k = 1
    while k < sub:
        bc = bc + jnp.where(rmod >= k, shift(bc, k), 0.0)
        k *= 2
    o_d = [jnp.zeros((tile, hd), F32) for _ in range(HG_HEADS)]
    for d in range(sub):
        if d == 0:
            p = qf * kf
            vd = vi
        else:
            diff = jnp.where(rmod >= d, bc - shift(bc, d), -jnp.inf)
            p = qf * shift(kf, d) * jnp.exp(diff)
            vd = shift(vi, d)
        for h in range(HG_HEADS):
            att = jnp.sum(p[:, h * hd:(h + 1) * hd], axis=-1, keepdims=True)
            o_d[h] = o_d[h] + att * vd[:, h * hd:(h + 1) * hd]
    qe = qf * jnp.exp(bc)
    for j in range(tile // sub):
        lo, hi = j * sub, (j + 1) * sub
        b_last = bc[hi - 1:hi]
        kdec = kf[lo:hi] * jnp.exp(b_last - bc[lo:hi])
        decay = jnp.exp(b_last)
        inter = []
        for h in range(HG_HEADS):
            st = st_sc[h]
            inter.append(_bdot_nt(qe[lo:hi, h * hd:(h + 1) * hd], st))
            upd = _bdot_tn(vi[lo:hi, h * hd:(h + 1) * hd], kdec[:, h * hd:(h + 1) * hd])
            st_sc[h] = st * decay[:, h * hd:(h + 1) * hd] + upd
        inter = jnp.concatenate(inter, axis=1)
        if j == 0:
            o_inter = [inter]
        else:
            o_inter.append(inter)
    o_inter = jnp.concatenate(o_inter, axis=0) if len(o_inter) > 1 else o_inter[0]
    gn = hgn_ref[...]
    o_n = []
    for h in range(HG_HEADS):
        oh = o_d[h] + o_inter[:, h * hd:(h + 1) * hd]
        ms = jnp.mean(oh * oh, axis=-1, keepdims=True)
        o_n.append(oh * lax.rsqrt(ms + RMS_EPS) * gn)
    o_d = jnp.concatenate(o_n, axis=1) * (go * _sigmoid(go))

    obr_ref[...] = jnp.concatenate([o_a, o_b, o_c, o_d], axis=1).astype(BF16)

    @pl.when(t == n_t - 1)
    def _emit_state():
        for h in range(HG_HEADS):
            sn_ref[h] = st_sc[h].T


def _seqmix(x, p, lb, states, *, batch, seq, tile, sub, pos0, emit_v):
    n, d = x.shape
    br = d // 2
    hd = br // HG_HEADS
    n_t = seq // tile
    has_state = states is not None
    mixw = 9 * br
    row = lambda a: a.reshape(1, -1)
    ins = [x, p["w_in_mix"], row(p["b_in"][:mixw]), row(p["sgu_ln_g"]), row(p["sgu_ln_b"]), p["sgu_w"],
           p["sgu_b"].T, p["conv_w"], row(p["conv_b"]), p["lru_wr"], row(p["lru_br"]), p["lru_wi"], row(p["lru_bi"]),
           row(p["lru_lambda"]), p["pool_w"], row(p["pool_b"]), row(p["pool_scale"]), row(lb), row(p["hgrn_norm_g"])]
    in_specs = [pl.BlockSpec((tile, d), lambda b, t: (b * n_t + t, 0))] + [_resident(a.shape) for a in ins[1:]]
    if has_state:
        conv0, h0, pool0, s0 = states
        ins += [conv0, h0.reshape(batch, 1, br), pool0, s0]
        in_specs += [
            pl.BlockSpec((None, CONV_W - 1, br), lambda b, t: (b, 0, 0)),
            pl.BlockSpec((None, 1, br), lambda b, t: (b, 0, 0)),
            pl.BlockSpec((None, POOL_BUF, br), lambda b, t: (b, 0, 0)),
            pl.BlockSpec((None, HG_HEADS, hd, hd), lambda b, t: (b, 0, 0, 0)),
        ]
    out_shape = [
        jax.ShapeDtypeStruct((n, N_BRANCH * br), BF16),
        jax.ShapeDtypeStruct((batch, CONV_W - 1, br), F32),
        jax.ShapeDtypeStruct((batch, 1, br), F32),
        jax.ShapeDtypeStruct((batch, POOL_BUF, br), F32),
        jax.ShapeDtypeStruct((batch, HG_HEADS, hd, hd), F32),
    ]
    out_specs = [
        pl.BlockSpec((tile, N_BRANCH * br), lambda b, t: (b * n_t + t, 0)),
        pl.BlockSpec((None, CONV_W - 1, br), lambda b, t: (b, 0, 0)),
        pl.BlockSpec((None, 1, br), lambda b, t: (b, 0, 0)),
        pl.BlockSpec((None, POOL_BUF, br), lambda b, t: (b, 0, 0)),
        pl.BlockSpec((None, HG_HEADS, hd, hd), lambda b, t: (b, 0, 0, 0)),
    ]
    if emit_v:
        out_shape.append(jax.ShapeDtypeStruct((n, br), F32))
        out_specs.append(pl.BlockSpec((tile, br), lambda b, t: (b * n_t + t, 0)))
    scratch = [
        pltpu.VMEM((SUBLANES, br), F32),
        pltpu.VMEM((POOL_BUF + 1, br), F32),
        pltpu.VMEM((1, br), F32),
        pltpu.VMEM((HG_HEADS, hd, hd), F32),
    ]
    kern = functools.partial(_seqmix_kernel, tile=tile, sub=sub, pos0=pos0, has_state=has_state, emit_v=emit_v, br=br)
    outs = pl.pallas_call(
        kern,
        grid=(batch, n_t),
        in_specs=in_specs,
        out_specs=out_specs,
        out_shape=out_shape,
        scratch_shapes=scratch,
        compiler_params=_params(2),
        name="seqmix_state" if has_state else "seqmix",
    )(*ins)
    return outs


def _merge_kernel(x_ref, obr_ref, wg_ref, bg_ref, wb_ref, wo_ref, lng_ref, lnb_ref, wq_ref, x1_ref, q_ref, *, br, d):
    x = x_ref[...]
    xb16 = x.astype(BF16)
    acc = None
    for g in range(N_BRANCH):
        gz = jnp.dot(xb16, wg_ref[:, g * d:(g + 1) * d], preferred_element_type=F32) + bg_ref[:, g * d:(g + 1) * d]
        proj = jnp.dot(obr_ref[:, g * br:(g + 1) * br], wb_ref[g], preferred_element_type=F32)
        term = _sigmoid(gz) * proj
        acc = term if acc is None else acc + term
    y = _bdot(acc, wo_ref[...])
    x1 = _layer_norm(DN_ALPHA * x + y, lng_ref[...], lnb_ref[...])
    x1_ref[...] = x1
    q_ref[...] = _bdot(x1, wq_ref[...]).astype(BF16)


def _merge(x, obr, p, tile):
    n, d = x.shape
    br = d // 2
    ins = [x, obr, p["w_in_gate"], p["b_in"][9 * br:].reshape(1, -1), p["w_branch"], p["w_out"],
           p["ln_g"][0].reshape(1, d), p["ln_b"][0].reshape(1, d), p["x_wq"]]
    in_specs = [pl.BlockSpec((tile, d), lambda i: (i, 0)), pl.BlockSpec((tile, N_BRANCH * br), lambda i: (i, 0))]
    in_specs += [_resident(a.shape) for a in ins[2:]]
    return pl.pallas_call(
        functools.partial(_merge_kernel, br=br, d=d),
        grid=(n // tile,),
        in_specs=in_specs,
        out_specs=[pl.BlockSpec((tile, d), lambda i: (i, 0))] * 2,
        out_shape=[jax.ShapeDtypeStruct((n, d), F32), jax.ShapeDtypeStruct((n, d), BF16)],
        compiler_params=_params(1),
        name="merge",
    )(*ins)


def _attn_kernel(q_ref, k_ref, v_ref, o_ref, *, hd):
    scale = hd ** -0.5
    outs = []
    for h in range(X_HEADS):
        sl = slice(h * hd, (h + 1) * hd)
        s = _bdot_nt(q_ref[:, sl], k_ref[:, sl]) * scale
        e = jnp.exp(s - jnp.max(s, axis=-1, keepdims=True))
        prob = e / jnp.sum(e, axis=-1, keepdims=True)
        outs.append(_bdot(prob, v_ref[:, sl]))
    o_ref[...] = jnp.concatenate(outs, axis=1).astype(BF16)


def _attn(q, mem_k, mem_v, layer, *, batch, seq, tile):
    n, d = q.shape
    n_t = seq // tile
    m = mem_k.shape[2]
    kv_spec = pl.BlockSpec((None, None, m, d), lambda b, t: (layer, b, 0, 0))
    return pl.pallas_call(
        functools.partial(_attn_kernel, hd=d // X_HEADS),
        grid=(batch, n_t),
        in_specs=[pl.BlockSpec((tile, d), lambda b, t: (b * n_t + t, 0)), kv_spec, kv_spec],
        out_specs=pl.BlockSpec((tile, d), lambda b, t: (b * n_t + t, 0)),
        out_shape=jax.ShapeDtypeStruct((n, d), BF16),
        compiler_params=_params(2),
        name="attn",
    )(q, mem_k, mem_v)


def _attn_cache_kernel(q_ref, k_hbm, v_hbm, o_ref, kbuf, vbuf, sem, *, layer, hd, rows, seq):
    g = pl.program_id(0)
    n_g = pl.num_programs(0)
    slot = lax.rem(g, 2)

    def head_copies(step, buf_slot):
        out = []
        src_rows = pl.ds(step * rows, rows)
        for h in range(X_HEADS):
            lanes = pl.ds(h * hd, hd)
            out.append(pltpu.make_async_copy(k_hbm.at[layer, src_rows, :, h, :], kbuf.at[buf_slot, :, :, lanes],
                                             sem.at[0, buf_slot, h]))
            out.append(pltpu.make_async_copy(v_hbm.at[layer, src_rows, :, h, :], vbuf.at[buf_slot, :, :, lanes],
                                             sem.at[1, buf_slot, h]))
        return out

    @pl.when(g == 0)
    def _first():
        for cp in head_copies(0, 0):
            cp.start()

    @pl.when(g + 1 < n_g)
    def _prefetch():
        for cp in head_copies(g + 1, 1 - slot):
            cp.start()

    for cp in head_copies(g, slot):
        cp.wait()

    scale = hd ** -0.5
    for r in range(rows):
        outs = []
        for h in range(X_HEADS):
            sl = slice(h * hd, (h + 1) * hd)
            s = _bdot_nt(q_ref[r * seq:(r + 1) * seq, sl], kbuf[slot, r, :, sl]) * scale
            e = jnp.exp(s - jnp.max(s, axis=-1, keepdims=True))
            prob = e / jnp.sum(e, axis=-1, keepdims=True)
            outs.append(_bdot(prob, vbuf[slot, r, :, sl]))
        o_ref[r * seq:(r + 1) * seq, :] = jnp.concatenate(outs, axis=1).astype(BF16)


def _attn_cache(q, cache_k, cache_v, layer, *, batch, seq, rows):
    n, d = q.shape
    m, hd = cache_k.shape[2], cache_k.shape[4]
    return pl.pallas_call(
        functools.partial(_attn_cache_kernel, layer=layer, hd=hd, rows=rows, seq=seq),
        grid=(batch // rows,),
        in_specs=[pl.BlockSpec((rows * seq, d), lambda g: (g, 0)),
                  pl.BlockSpec(memory_space=pl.ANY), pl.BlockSpec(memory_space=pl.ANY)],
        out_specs=pl.BlockSpec((rows * seq, d), lambda g: (g, 0)),
        out_shape=jax.ShapeDtypeStruct((n, d), BF16),
        scratch_shapes=[pltpu.VMEM((2, rows, m, d), F32), pltpu.VMEM((2, rows, m, d), F32),
                        pltpu.SemaphoreType.DMA((2, 2, X_HEADS))],
        compiler_params=_params(1),
        name="attn_cache",
    )(q, cache_k, cache_v)


def _route(s, sel):
    neg = -jnp.inf

    def top2(vals):
        m1 = functools.reduce(jnp.maximum, vals)
        i1 = jnp.full(m1.shape, len(vals) - 1, jnp.int32)
        for j in range(len(vals) - 2, -1, -1):
            i1 = jnp.where(vals[j] == m1, j, i1)
        rest = [jnp.where(i1 == j, neg, vals[j]) for j in range(len(vals))]
        m2 = functools.reduce(jnp.maximum, rest)
        i2 = jnp.full(m1.shape, len(vals) - 1, jnp.int32)
        for j in range(len(vals) - 2, -1, -1):
            i2 = jnp.where(rest[j] == m2, j, i2)
        return m1, m2, i1, i2

    groups = [top2(sel[g * EXPERTS_PER_GROUP:(g + 1) * EXPERTS_PER_GROUP]) for g in range(N_GROUPS)]
    score = [m1 + m2 for (m1, m2, _, _) in groups]
    best = functools.reduce(jnp.maximum, score)
    g_idx = jnp.full(best.shape, N_GROUPS - 1, jnp.int32)
    for g in range(N_GROUPS - 2, -1, -1):
        g_idx = jnp.where(score[g] == best, g, g_idx)
    e1 = jnp.zeros(best.shape, jnp.int32)
    e2 = jnp.zeros(best.shape, jnp.int32)
    for g in range(N_GROUPS):
        e1 = jnp.where(g_idx == g, g * EXPERTS_PER_GROUP + groups[g][2], e1)
        e2 = jnp.where(g_idx == g, g * EXPERTS_PER_GROUP + groups[g][3], e2)
    w1 = functools.reduce(jnp.add, [jnp.where(e1 == e, s[e], 0.0) for e in range(N_EXPERTS)])
    w2 = functools.reduce(jnp.add, [jnp.where(e2 == e, s[e], 0.0) for e in range(N_EXPERTS)])
    tot = w1 + w2
    return e1, e2, w1 / tot, w2 / tot


def _post_kernel(x1_ref, o_ref, wo_ref, lng_ref, lnb_ref, rwt_ref, rb_ref, tri_ref,
                 x2_ref, x2b_ref, rows_ref, cols_ref, seg_ref, *, tile):
    y = jnp.dot(o_ref[...], wo_ref[...], preferred_element_type=F32)
    x2 = _layer_norm(DN_ALPHA * x1_ref[...] + y, lng_ref[...], lnb_ref[...])
    x2_ref[...] = x2
    x2b_ref[...] = x2.astype(BF16)
    logits = lax.dot_general(rwt_ref[...], x2, (((1,), (1,)), ((), ())), preferred_element_type=F32,
                             precision=lax.Precision.HIGHEST)
    s_all = _sigmoid(logits)
    sel_all = s_all + rb_ref[...]
    s = [s_all[e:e + 1] for e in range(N_EXPERTS)]
    sel = [sel_all[e:e + 1] for e in range(N_EXPERTS)]
    e1, e2, w1, w2 = _route(s, sel)

    onehot = jnp.concatenate([jnp.where((e1 == e) | (e2 == e), 1.0, 0.0) for e in range(N_EXPERTS)], axis=0)
    prefix = jnp.dot(onehot.astype(BF16), tri_ref[...], preferred_element_type=F32)
    cnt = jnp.broadcast_to(jnp.sum(onehot, axis=-1, keepdims=True), (N_EXPERTS, LANES))
    cpad = jnp.floor((cnt + (SEG_ALIGN - 1.0)) * (1.0 / SEG_ALIGN)) * SEG_ALIGN
    ridx = lax.broadcasted_iota(jnp.int32, (N_EXPERTS, LANES), 0)
    inc = cpad
    k = 1
    while k < N_EXPERTS:
        inc = inc + jnp.where(ridx >= k, _shift_rows(inc, k), 0.0)
        k *= 2
    start = inc - cpad

    def pick(e_row, table):
        return functools.reduce(jnp.add, [jnp.where(e_row == e, table[e:e + 1], 0.0) for e in range(N_EXPERTS)])

    start_col = start[:, 0:1]
    pos1 = pick(e1, start_col) + pick(e1, prefix)
    pos2 = pick(e2, start_col) + pick(e2, prefix)
    rows = jnp.concatenate([pos1, pos2, w1, w2, jnp.zeros((SUBLANES - 4, tile), F32)], axis=0)
    rows_ref[...] = rows
    padded = jnp.concatenate([rows, jnp.zeros((LANES - SUBLANES, tile), F32)], axis=0)
    for j in range(tile // LANES):
        cols_ref[j * LANES:(j + 1) * LANES, :] = padded[:, j * LANES:(j + 1) * LANES].T
    seg_ref[0] = start
    seg_ref[1] = cnt


def _post(x1, o, p, router_wt, router_b, tri, tile):
    n, d = x1.shape
    ins = [x1, o, p["x_wo"], p["ln_g"][1].reshape(1, d), p["ln_b"][1].reshape(1, d), router_wt,
           router_b.reshape(N_EXPERTS, 1), tri]
    in_specs = [pl.BlockSpec((tile, d), lambda i: (i, 0))] * 2 + [_resident(a.shape) for a in ins[2:]]
    return pl.pallas_call(
        functools.partial(_post_kernel, tile=tile),
        grid=(n // tile,),
        in_specs=in_specs,
        out_specs=[pl.BlockSpec((tile, d), lambda i: (i, 0))] * 2 + [
            pl.BlockSpec((SUBLANES, tile), lambda i: (0, i)),
            pl.BlockSpec((tile, LANES), lambda i: (i, 0)),
            pl.BlockSpec((None, 2, N_EXPERTS, LANES), lambda i: (i, 0, 0, 0))],
        out_shape=[jax.ShapeDtypeStruct((n, d), F32), jax.ShapeDtypeStruct((n, d), BF16),
                   jax.ShapeDtypeStruct((SUBLANES, n), F32), jax.ShapeDtypeStruct((n, LANES), F32),
                   jax.ShapeDtypeStruct((n // tile, 2, N_EXPERTS, LANES), F32)],
        compiler_params=_params(1),
        name="post_attn_router",
    )(*ins)


def _moe_kernel(start_ref, cnt_ref, x2_ref, xb_ref, rows_ref, cols_ref, wg_ref, wu_ref, wd_ref, lng_ref, lnb_ref,
                out_ref, xs_ref, ys_ref, ws_ref, *, tile, rcap, rblk, tblk):
    i = pl.program_id(0)
    e = pl.program_id(1)

    @pl.when(e == 0)
    def _sort_rows():
        pos1, pos2 = rows_ref[0:1, :], rows_ref[1:2, :]
        w1, w2 = rows_ref[2:3, :], rows_ref[3:4, :]
        xb = xb_ref[...]
        for c in range(rcap // rblk):
            r = (c * rblk + lax.broadcasted_iota(jnp.int32, (rblk, tile), 0)).astype(F32)
            m1 = pos1 == r
            m2 = pos2 == r
            sel = jnp.where(m1 | m2, 1.0, 0.0).astype(BF16)
            xs_ref[c * rblk:(c + 1) * rblk, :] = jnp.dot(sel, xb, preferred_element_type=F32).astype(BF16)
            ws_ref[c * rblk:(c + 1) * rblk, :] = jnp.sum(jnp.where(m1, w1, 0.0) + jnp.where(m2, w2, 0.0),
                                                       axis=-1, keepdims=True)
        ys_ref[...] = jnp.zeros_like(ys_ref)

    start = start_ref[i * N_EXPERTS + e]
    n_chunks = cnt_ref[i * N_EXPERTS + e]

    def chunk_body(j, carry):
        r = pl.multiple_of(start + j * MOE_CHUNK, SEG_ALIGN)
        xs = xs_ref[pl.ds(r, MOE_CHUNK), :]
        gate = jnp.dot(xs, wg_ref[...], preferred_element_type=F32)
        up = jnp.dot(xs, wu_ref[...], preferred_element_type=F32)
        hid = gate * _sigmoid(gate) * up
        y = _bdot(hid, wd_ref[...])
        ys_ref[pl.ds(r, MOE_CHUNK), :] = (ws_ref[pl.ds(r, MOE_CHUNK), :] * y).astype(BF16)
        return carry

    lax.fori_loop(0, n_chunks, chunk_body, 0)

    @pl.when(e == pl.num_programs(1) - 1)
    def _finish():
        for c in range(tile // tblk):
            sl = slice(c * tblk, (c + 1) * tblk)
            r = lax.broadcasted_iota(jnp.int32, (tblk, rcap), 1).astype(F32)
            back = jnp.where((cols_ref[sl, 0:1] == r) | (cols_ref[sl, 1:2] == r), 1.0, 0.0).astype(BF16)
            y = jnp.dot(back, ys_ref[...], preferred_element_type=F32)
            out_ref[sl, :] = _layer_norm(DN_ALPHA * x2_ref[sl, :] + y, lng_ref[...], lnb_ref[...])


def _moe(x2, x2b, rows, cols, seg, p, tile):
    n, d = x2.shape
    de = p["e_wg"].shape[-1]
    tblk = min(256, tile)
    rcap = -(-(2 * tile + N_EXPERTS * SEG_ALIGN + MOE_CHUNK) // SEG_ALIGN) * SEG_ALIGN
    rblk = max(b for b in range(SEG_ALIGN, 257, SEG_ALIGN) if rcap % b == 0)
    seg_start = seg[:, 0, :, 0].astype(jnp.int32).reshape(-1)
    seg_cnt = (seg[:, 1, :, 0].astype(jnp.int32).reshape(-1) + (MOE_CHUNK - 1)) // MOE_CHUNK
    row_spec = lambda w: pl.BlockSpec((tile, w), lambda i, e, s, c: (i, 0))
    grid_spec = pltpu.PrefetchScalarGridSpec(
        num_scalar_prefetch=2,
        grid=(n // tile, N_EXPERTS),
        in_specs=[row_spec(d), row_spec(d),
                  pl.BlockSpec((SUBLANES, tile), lambda i, e, s, c: (0, i)),
                  row_spec(LANES),
                  pl.BlockSpec((None, d, de), lambda i, e, s, c: (e, 0, 0)),
                  pl.BlockSpec((None, d, de), lambda i, e, s, c: (e, 0, 0)),
                  pl.BlockSpec((None, de, d), lambda i, e, s, c: (e, 0, 0)),
                  pl.BlockSpec((1, d), lambda i, e, s, c: (0, 0)),
                  pl.BlockSpec((1, d), lambda i, e, s, c: (0, 0))],
        out_specs=row_spec(d),
        scratch_shapes=[pltpu.VMEM((rcap, d), BF16), pltpu.VMEM((rcap, d), BF16), pltpu.VMEM((rcap, 1), F32)],
    )
    return pl.pallas_call(
        functools.partial(_moe_kernel, tile=tile, rcap=rcap, rblk=rblk, tblk=tblk),
        grid_spec=grid_spec,
        out_shape=jax.ShapeDtypeStruct((n, d), F32),
        compiler_params=_params(2),
        name="moe",
    )(seg_start, seg_cnt, x2, x2b, rows, cols, p["e_wg"], p["e_wu"], p["e_wd"],
      p["ln_g"][2].reshape(1, d), p["ln_b"][2].reshape(1, d))


def _pick(n, pref):
    return pref if n % pref == 0 else n


def kernel(x_prompt, x_sample, mem_prompt, cache_mem_k, cache_mem_v, state_conv, state_rglru, state_pool, state_hgrn, ln_in_g, ln_in_b, w_in, b_in, sgu_ln_g, sgu_ln_b, sgu_w, sgu_b, conv_w, conv_b, lru_wr, lru_br, lru_wi, lru_bi, lru_lambda, pool_w, pool_b, pool_scale, hgrn_lb, hgrn_norm_g, w_branch, w_out, x_wq, x_wk, x_wv, x_wo, ln_g, ln_b, router_w, router_b, e_wg, e_wu, e_wd):
    bp, tp, d = x_prompt.shape
    bs, ts, _ = x_sample.shape
    depth = w_in.shape[0]
    br = d // 2
    n_mem = mem_prompt.shape[1]
    hd = br // HG_HEADS
    mixw = 9 * br
    assert depth == MODEL_DEPTH

    lb_cum = jnp.cumsum(jax.nn.softmax(hgrn_lb.astype(F32), axis=0), axis=0)
    lbs = lb_cum - lb_cum[0]

    w_in_mix = w_in[:, :, :mixw].astype(BF16)
    w_in_gate = w_in[:, :, mixw:].astype(BF16)
    wts = dict(w_branch=w_branch, w_out=w_out, x_wq=x_wq, x_wo=x_wo, lru_wr=lru_wr, lru_wi=lru_wi, pool_w=pool_w,
               e_wg=e_wg, e_wu=e_wu, e_wd=e_wd)
    wts = {k: v.astype(BF16) for k, v in wts.items()}
    router_wt = router_w.T

    tile_p = SGU_CHUNK if tp % SGU_CHUNK == 0 else tp
    tile_s = SGU_CHUNK if ts % SGU_CHUNK == 0 else ts
    sub_p = _pick(tile_p, SUBLANES)
    sub_s = _pick(tile_s, SUBLANES)

    n_p, n_s = bp * tp, bs * ts
    xp = _input_ln(x_prompt.reshape(n_p, d), ln_in_g, ln_in_b, _pick(n_p, 512))
    xs = _input_ln(x_sample.reshape(n_s, d), ln_in_g, ln_in_b, _pick(n_s, 512))

    mem_flat = mem_prompt.reshape(bp * n_mem, d)
    mk_all, mv_all, mkb, mvb = _memory_kv(mem_flat, x_wk.astype(BF16), x_wv.astype(BF16), _pick(bp * n_mem, 512))
    mk_all = mk_all.reshape(depth, bp, n_mem, X_HEADS, d // X_HEADS)
    mv_all = mv_all.reshape(depth, bp, n_mem, X_HEADS, d // X_HEADS)
    mkb = mkb.reshape(depth, bp, n_mem, d)
    mvb = mvb.reshape(depth, bp, n_mem, d)

    moe_tile_p = _pick(n_p, MOE_TILE)
    moe_tile_s = _pick(n_s, MOE_TILE)

    def strict_upper(t):
        idx = jnp.arange(t)
        return (idx[:, None] < idx[None, :]).astype(BF16)

    tri_p, tri_s = strict_upper(moe_tile_p), strict_upper(moe_tile_s)

    outs_p, outs_s = [], []
    for l in range(depth):
        p = dict(w_in_mix=w_in_mix[l], w_in_gate=w_in_gate[l], b_in=b_in[l], sgu_ln_g=sgu_ln_g[l],
                 sgu_ln_b=sgu_ln_b[l], sgu_w=sgu_w[l], sgu_b=sgu_b[l], conv_w=conv_w[l], conv_b=conv_b[l],
                 lru_br=lru_br[l], lru_bi=lru_bi[l], lru_lambda=lru_lambda[l], pool_b=pool_b[l],
                 pool_scale=pool_scale[l], hgrn_norm_g=hgrn_norm_g[l], ln_g=ln_g[l], ln_b=ln_b[l])
        p.update({k: v[l] for k, v in wts.items()})

        obr_p, conv_p, h_p, pool_p, s_p = _seqmix(xp, p, lbs[l], None, batch=bp, seq=tp, tile=tile_p, sub=sub_p,
                                                   pos0=0, emit_v=False)
        obr_s, conv_s, h_s, pool_s, s_s, v_s = _seqmix(
            xs, p, lbs[l], (state_conv[l], state_rglru[l], state_pool[l], state_hgrn[l]),
            batch=bs, seq=ts, tile=tile_s, sub=sub_s, pos0=PAST_LEN, emit_v=True)
        outs_p.append((conv_p, h_p.reshape(bp, br), pool_p, s_p))
        outs_s.append((conv_s, h_s.reshape(bs, br), pool_s, s_s, v_s.reshape(bs, ts, br)))

        x1p, qp = _merge(xp, obr_p, p, _pick(n_p, 256))
        x1s, qs = _merge(xs, obr_s, p, _pick(n_s, 256))
        op = _attn(qp, mkb, mvb, l, batch=bp, seq=tp, tile=_pick(tp, 256))
        os_ = _attn_cache(qs, cache_mem_k, cache_mem_v, l, batch=bs, seq=ts, rows=_pick(bs, CACHE_ROWS))
        x2p, x2pb, rows_p, cols_p, seg_p = _post(x1p, op, p, router_wt, router_b, tri_p, moe_tile_p)
        x2s, x2sb, rows_s, cols_s, seg_s = _post(x1s, os_, p, router_wt, router_b, tri_s, moe_tile_s)
        xp = _moe(x2p, x2pb, rows_p, cols_p, seg_p, p, moe_tile_p)
        xs = _moe(x2s, x2sb, rows_s, cols_s, seg_s, p, moe_tile_s)

    stack = lambda outs, i: jnp.stack([o[i] for o in outs])
    return (xp.reshape(bp, tp, d), xs.reshape(bs, ts, d), mk_all, mv_all,
            stack(outs_p, 0), stack(outs_p, 1), stack(outs_p, 2), stack(outs_p, 3),
            stack(outs_s, 0), stack(outs_s, 1), stack(outs_s, 2), stack(outs_s, 3), stack(outs_s, 4))
```

```python
import functools

import jax
import jax.numpy as jnp
from jax import lax
from jax.experimental import pallas as pl
from jax.experimental.pallas import tpu as pltpu

F32 = jnp.float32
BF16 = jnp.bfloat16

N_BRANCH = 4
SGU_GROUPS = 4
SGU_CHUNK = 128
LRU_HEADS = 4
CONV_W = 4
LRU_C = 8.0
POOL_WINDOWS = (2, 4, 8, 16)
POOL_BUF = max(POOL_WINDOWS) - 1
HG_HEADS = 4
X_HEADS = 4
N_EXPERTS = 16
N_GROUPS = 4
EXPERTS_PER_GROUP = N_EXPERTS // N_GROUPS
MODEL_DEPTH = 4
DN_ALPHA = (2 * MODEL_DEPTH) ** 0.25
LN_EPS = 1e-5
RMS_EPS = 1e-6
PAST_LEN = 16384
MOE_TILE = 1024
MOE_CHUNK = 160
SEG_ALIGN = 16
CACHE_ROWS = 4

V7X_VMEM_BYTES = 64 * 1024 * 1024
VMEM_LIMIT = V7X_VMEM_BYTES - 8 * 1024 * 1024
SUBLANES = 8
LANES = 128


def _params(n_grid):
    return pltpu.CompilerParams(dimension_semantics=("arbitrary",) * n_grid, vmem_limit_bytes=VMEM_LIMIT)


def _resident(shape):
    nd = len(shape)
    return pl.BlockSpec(shape, lambda *_: (0,) * nd, pipeline_mode=pl.Buffered(1))


def _layer_spec(arr, layer):
    nd = arr.ndim - 1
    return pl.BlockSpec((None,) + tuple(arr.shape[1:]), lambda *_: (layer,) + (0,) * nd, pipeline_mode=pl.Buffered(1))


def _layer_norm(x, g, b):
    mu = jnp.mean(x, axis=-1, keepdims=True)
    xc = x - mu
    var = jnp.mean(xc * xc, axis=-1, keepdims=True)
    return xc * lax.rsqrt(var + LN_EPS) * g + b


def _sigmoid(x):
    return 1.0 / (1.0 + jnp.exp(-x))


def _bdot(a, b):
    return jnp.dot(a.astype(BF16), b.astype(BF16), preferred_element_type=F32)


def _bdot_nt(a, b):
    return lax.dot_general(a.astype(BF16), b.astype(BF16), (((1,), (1,)), ((), ())), preferred_element_type=F32)


def _bdot_tn(a, b):
    return lax.dot_general(a.astype(BF16), b.astype(BF16), (((0,), (0,)), ((), ())), preferred_element_type=F32)


def _shift_rows(x, k):
    return pltpu.roll(x, k, 0)


def _shift_rows_fill(x, k, fill):
    n, c = x.shape
    if k % SUBLANES == 0:
        return jnp.concatenate([jnp.full((k, c), fill, x.dtype), x[:n - k]], axis=0)
    row = lax.broadcasted_iota(jnp.int32, x.shape, 0)
    return jnp.where(row >= k, pltpu.roll(x, k, 0), fill)


def _rotate_in_groups(x, k):
    n, c = x.shape
    return pltpu.roll(x.reshape(n // SUBLANES, SUBLANES, c), k, 1).reshape(n, c)


def _ln_kernel(x_ref, g_ref, b_ref, o_ref):
    o_ref[...] = _layer_norm(x_ref[...], g_ref[...], b_ref[...])


def _input_ln(x, g, b, tile):
    n, d = x.shape
    return pl.pallas_call(
        _ln_kernel,
        grid=(n // tile,),
        in_specs=[pl.BlockSpec((tile, d), lambda i: (i, 0)), _resident((1, d)), _resident((1, d))],
        out_specs=pl.BlockSpec((tile, d), lambda i: (i, 0)),
        out_shape=jax.ShapeDtypeStruct((n, d), F32),
        compiler_params=_params(1),
        name="input_ln",
    )(x, g.reshape(1, d), b.reshape(1, d))


def _kv_kernel(m_ref, wk_ref, wv_ref, k_ref, v_ref, kb_ref, vb_ref, *, hd):
    m = m_ref[...].astype(BF16)
    k = jnp.dot(m, wk_ref[...], preferred_element_type=F32)
    v = jnp.dot(m, wv_ref[...], preferred_element_type=F32)
    for h in range(X_HEADS):
        k_ref[:, h, :] = k[:, h * hd:(h + 1) * hd]
        v_ref[:, h, :] = v[:, h * hd:(h + 1) * hd]
    kb_ref[...] = k.astype(BF16)
    vb_ref[...] = v.astype(BF16)


def _memory_kv(mem, wk, wv, tile):
    n, d = mem.shape
    depth = wk.shape[0]
    hd = d // X_HEADS
    out = jax.ShapeDtypeStruct((depth, n, X_HEADS, hd), F32)
    outb = jax.ShapeDtypeStruct((depth, n, d), BF16)
    return pl.pallas_call(
        functools.partial(_kv_kernel, hd=hd),
        grid=(depth, n // tile),
        in_specs=[
            pl.BlockSpec((tile, d), lambda l, i: (i, 0)),
            pl.BlockSpec((None, d, d), lambda l, i: (l, 0, 0)),
            pl.BlockSpec((None, d, d), lambda l, i: (l, 0, 0)),
        ],
        out_specs=[pl.BlockSpec((None, tile, X_HEADS, hd), lambda l, i: (l, i, 0, 0))] * 2
        + [pl.BlockSpec((None, tile, d), lambda l, i: (l, i, 0))] * 2,
        out_shape=[out, out, outb, outb],
        compiler_params=_params(2),
        name="memory_kv",
    )(mem, wk, wv)


def _project_kernel(x_ref, w_ref, b_ref, z_ref, *, width):
    z_ref[...] = jnp.dot(x_ref[...].astype(BF16), w_ref[:, :width], preferred_element_type=F32) + b_ref[...]


def _project(x, p, tile):
    n, d = x.shape
    wts, layer = p["wts"], p["layer"]
    width = 9 * (d // 2)
    return pl.pallas_call(
        functools.partial(_project_kernel, width=width),
        grid=(n // tile,),
        in_specs=[pl.BlockSpec((tile, d), lambda i: (i, 0)), _layer_spec(wts["w_in"], layer), _resident((1, width))],
        out_specs=pl.BlockSpec((tile, width), lambda i: (i, 0)),
        out_shape=jax.ShapeDtypeStruct((n, width), F32),
        compiler_params=_params(1),
        name="project",
    )(x, wts["w_in"], p["b_in"][:width].reshape(1, width))


def _seqmix_kernel(*refs, tile, sub, pos0, has_state, emit_v, br, projected):
    if projected:
        z_ref = refs[0]
        refs = refs[1:]
    else:
        x_ref, win_ref, bin_ref = refs[:3]
        refs = refs[3:]
    (sgug_ref, sgub_ref, sguw_ref, sgubt_ref, convw_ref, convb_ref, wr_ref, br_ref,
     wi_ref, bi_ref, lam_ref, poolw_ref, poolb_ref, pools_ref, lb_ref, hgn_ref) = refs[:16]
    refs = refs[16:]
    if has_state:
        conv0_ref, h0_ref, pool0_ref, s0_ref = refs[:4]
        refs = refs[4:]
    obr_ref, convn_ref, hn_ref, pooln_ref, sn_ref = refs[:5]
    refs = refs[5:]
    if emit_v:
        vrows_ref = refs[0]
        refs = refs[1:]
    conv_sc, pool_sc, h_sc, st_sc = refs

    t = pl.program_id(1)
    n_t = pl.num_programs(1)
    gd = br // SGU_GROUPS
    hd = br // HG_HEADS

    @pl.when(t == 0)
    def _init():
        conv_sc[...] = jnp.zeros_like(conv_sc)
        pool_sc[...] = jnp.zeros_like(pool_sc)
        if has_state:
            conv_sc[SUBLANES - (CONV_W - 1):, :] = conv0_ref[...]
            pool_sc[1:, :] = pool0_ref[...]
            h_sc[...] = h0_ref[...]
            for h in range(HG_HEADS):
                st_sc[h] = s0_ref[h].T
        else:
            h_sc[...] = jnp.zeros_like(h_sc)
            st_sc[...] = jnp.zeros_like(st_sc)

    if projected:
        z = z_ref[...]
    else:
        z = jnp.dot(x_ref[...].astype(BF16), win_ref[:, :9 * br], preferred_element_type=F32) + bin_ref[...]
    u, v, yg, xb, c, q, f, vi, go = [z[:, k * br:(k + 1) * br] for k in range(9)]
    row = lax.broadcasted_iota(jnp.int32, (tile, br), 0)

    vn = _layer_norm(v, sgug_ref[...], sgub_ref[...])
    if emit_v:
        vrows_ref[...] = vn
    rr = lax.broadcasted_iota(jnp.int32, (tile, tile), 0)
    cc = lax.broadcasted_iota(jnp.int32, (tile, tile), 1)
    o_a = []
    for g in range(SGU_GROUPS):
        w = jnp.where(rr >= cc, sguw_ref[g, :tile, :tile], 0.0)
        mixed = _bdot(w, vn[:, g * gd:(g + 1) * gd]) + sgubt_ref[:tile, g:g + 1]
        o_a.append(u[:, g * gd:(g + 1) * gd] * mixed)
    o_a = jnp.concatenate(o_a, axis=1)

    ext = jnp.concatenate([conv_sc[...], xb], axis=0)
    cw = convw_ref[...]
    base = SUBLANES - (CONV_W - 1)
    xc = convb_ref[...] + xb * cw[CONV_W - 1:CONV_W]
    for j in range(CONV_W - 1):
        xc = xc + ext[base + j:base + j + tile] * cw[j:j + 1]
    conv_sc[...] = ext[tile:tile + SUBLANES]
    convn_ref[...] = ext[tile + base:tile + SUBLANES]

    def heads_dot(a, w_ref):
        return jnp.concatenate([_bdot(a[:, h * hd:(h + 1) * hd], w_ref[h]) for h in range(LRU_HEADS)], axis=1)

    r_g = _sigmoid(heads_dot(xc, wr_ref) + br_ref[...])
    i_g = _sigmoid(heads_dot(xc, wi_ref) + bi_ref[...])
    lam = lam_ref[...]
    softplus_neg_lam = jnp.maximum(-lam, 0.0) + jnp.log1p(jnp.exp(-jnp.abs(lam)))
    log_a = (-LRU_C) * r_g * softplus_neg_lam
    a = jnp.exp(log_a)
    y2 = 2.0 * log_a
    one_minus = jnp.tanh(-0.5 * y2) * (jnp.exp(y2) + 1.0)
    bx = jnp.sqrt(one_minus) * (i_g * xc)
    sa, sb = a, bx
    k = 1
    while k < tile:
        sb = sa * _shift_rows_fill(sb, k, 0.0) + sb
        sa = sa * _shift_rows_fill(sa, k, 1.0)
        k *= 2
    hseq = sb + sa * h_sc[...]
    h_sc[...] = hseq[tile - 1:tile]
    hn_ref[...] = hseq[tile - 1:tile]
    gelu = 0.5 * yg * (1.0 + jnp.tanh(0.7978845608028654 * (yg + 0.044715 * (yg * yg * yg))))
    o_b = gelu * hseq

    extp = jnp.concatenate([pool_sc[...], c], axis=0)
    pb = POOL_BUF + 1
    sums = {1: extp}
    wdt = 1
    while wdt < max(POOL_WINDOWS):
        prev = sums[wdt]
        sums[2 * wdt] = prev + (_shift_rows_fill(prev, wdt, 0.0) if wdt % SUBLANES == 0 else _shift_rows(prev, wdt))
        wdt *= 2
    tpos = pos0 + 1 + t * tile + lax.broadcasted_iota(jnp.int32, (tile, 1), 0)
    o_c = []
    for g, wdw in enumerate(POOL_WINDOWS):
        cnt = jnp.minimum(wdw, tpos).astype(F32)
        pooled = sums[wdw][pb:pb + tile, g * gd:(g + 1) * gd] / cnt
        o_c.append(_bdot(pooled - c[:, g * gd:(g + 1) * gd], poolw_ref[g]))
    o_c = (jnp.concatenate(o_c, axis=1) + poolb_ref[...]) * pools_ref[...]
    pool_sc[...] = extp[tile:tile + pb]
    pooln_ref[...] = extp[tile + 1:tile + pb]

    lb = lb_ref[...]
    log_sig = jnp.minimum(f, 0.0) - jnp.log1p(jnp.exp(-jnp.abs(f)))
    a1 = jnp.log(lb)
    a2 = jnp.log1p(-lb) + log_sig
    log_f = jnp.maximum(a1, a2) + jnp.log1p(jnp.exp(-jnp.abs(a1 - a2)))
    kf = (1.0 - lb) * _sigmoid(-f)
    qf = q * _sigmoid(q)
    rmod = row % sub
    shift = _rotate_in_groups if sub == SUBLANES else _shift_rows
    bc = log_f
    k = 1
    while k < sub:
        bc = bc + jnp.where(rmod >= k, shift(bc, k), 0.0)
        k *= 2
    o_d = [jnp.zeros((tile, hd), F32) for _ in range(HG_HEADS)]
    for d in range(sub):
        if d == 0:
            p = qf * kf
            vd = vi
        else:
            diff = jnp.where(rmod >= d, bc - shift(bc, d), -jnp.inf)
            p = qf * shift(kf, d) * jnp.exp(diff)
            vd = shift(vi, d)
        for h in range(HG_HEADS):
            att = jnp.sum(p[:, h * hd:(h + 1) * hd], axis=-1, keepdims=True)
            o_d[h] = o_d[h] + att * vd[:, h * hd:(h + 1) * hd]
    qe = qf * jnp.exp(bc)
    for j in range(tile // sub):
        lo, hi = j * sub, (j + 1) * sub
        b_last = bc[hi - 1:hi]
        kdec = kf[lo:hi] * jnp.exp(b_last - bc[lo:hi])
        decay = jnp.exp(b_last)
        inter = []
        for h in range(HG_HEADS):
            st = st_sc[h]
            inter.append(_bdot_nt(qe[lo:hi, h * hd:(h + 1) * hd], st))
            upd = _bdot_tn(vi[lo:hi, h * hd:(h + 1) * hd], kdec[:, h * hd:(h + 1) * hd])
            st_sc[h] = st * decay[:, h * hd:(h + 1) * hd] + upd
        inter = jnp.concatenate(inter, axis=1)
        if j == 0:
            o_inter = [inter]
        else:
            o_inter.append(inter)
    o_inter = jnp.concatenate(o_inter, axis=0) if len(o_inter) > 1 else o_inter[0]
    gn = hgn_ref[...]
    o_n = []
    for h in range(HG_HEADS):
        oh = o_d[h] + o_inter[:, h * hd:(h + 1) * hd]
        ms = jnp.mean(oh * oh, axis=-1, keepdims=True)
        o_n.append(oh * lax.rsqrt(ms + RMS_EPS) * gn)
    o_d = jnp.concatenate(o_n, axis=1) * (go * _sigmoid(go))

    obr_ref[...] = jnp.concatenate([o_a, o_b, o_c, o_d], axis=1).astype(BF16)

    @pl.when(t == n_t - 1)
    def _emit_state():
        for h in range(HG_HEADS):
            sn_ref[h] = st_sc[h].T


def _seqmix(x, p, lb, states, *, batch, seq, tile, sub, pos0, emit_v, projected):
    n = x.shape[0]
    wts, layer = p["wts"], p["layer"]
    d = wts["w_in"].shape[1]
    br = d // 2
    hd = br // HG_HEADS
    n_t = seq // tile
    has_state = states is not None
    mixw = 9 * br
    row = lambda a: a.reshape(1, -1)
    small = [row(p["sgu_ln_g"]), row(p["sgu_ln_b"]), p["sgu_w"], p["sgu_b"].T, p["conv_w"], row(p["conv_b"])]
    ins = [x]
    in_specs = [pl.BlockSpec((tile, x.shape[1]), lambda b, t: (b * n_t + t, 0))]
    if not projected:
        ins += [wts["w_in"], row(p["b_in"][:mixw])]
        in_specs += [_layer_spec(wts["w_in"], layer), _resident((1, mixw))]
    tail = [None, row(p["lru_br"]), None, row(p["lru_bi"]), row(p["lru_lambda"]), None, row(p["pool_b"]),
            row(p["pool_scale"]), row(lb), row(p["hgrn_norm_g"])]
    stacked = {0: "lru_wr", 2: "lru_wi", 5: "pool_w"}
    ins += small
    in_specs += [_resident(a.shape) for a in small]
    for i, a in enumerate(tail):
        if a is None:
            ins.append(wts[stacked[i]])
            in_specs.append(_layer_spec(wts[stacked[i]], layer))
        else:
            ins.append(a)
            in_specs.append(_resident(a.shape))
    if has_state:
        conv0, h0, pool0, s0 = states
        ins += [conv0, h0.reshape(batch, 1, br), pool0, s0]
        in_specs += [
            pl.BlockSpec((None, CONV_W - 1, br), lambda b, t: (b, 0, 0)),
            pl.BlockSpec((None, 1, br), lambda b, t: (b, 0, 0)),
            pl.BlockSpec((None, POOL_BUF, br), lambda b, t: (b, 0, 0)),
            pl.BlockSpec((None, HG_HEADS, hd, hd), lambda b, t: (b, 0, 0, 0)),
        ]
    out_shape = [
        jax.ShapeDtypeStruct((n, N_BRANCH * br), BF16),
        jax.ShapeDtypeStruct((batch, CONV_W - 1, br), F32),
        jax.ShapeDtypeStruct((batch, 1, br), F32),
        jax.ShapeDtypeStruct((batch, POOL_BUF, br), F32),
        jax.ShapeDtypeStruct((batch, HG_HEADS, hd, hd), F32),
    ]
    out_specs = [
        pl.BlockSpec((tile, N_BRANCH * br), lambda b, t: (b * n_t + t, 0)),
        pl.BlockSpec((None, CONV_W - 1, br), lambda b, t: (b, 0, 0)),
        pl.BlockSpec((None, 1, br), lambda b, t: (b, 0, 0)),
        pl.BlockSpec((None, POOL_BUF, br), lambda b, t: (b, 0, 0)),
        pl.BlockSpec((None, HG_HEADS, hd, hd), lambda b, t: (b, 0, 0, 0)),
    ]
    if emit_v:
        out_shape.append(jax.ShapeDtypeStruct((n, br), F32))
        out_specs.append(pl.BlockSpec((tile, br), lambda b, t: (b * n_t + t, 0)))
    scratch = [
        pltpu.VMEM((SUBLANES, br), F32),
        pltpu.VMEM((POOL_BUF + 1, br), F32),
        pltpu.VMEM((1, br), F32),
        pltpu.VMEM((HG_HEADS, hd, hd), F32),
    ]
    kern = functools.partial(_seqmix_kernel, tile=tile, sub=sub, pos0=pos0, has_state=has_state, emit_v=emit_v, br=br,
                             projected=projected)
    outs = pl.pallas_call(
        kern,
        grid=(batch, n_t),
        in_specs=in_specs,
        out_specs=out_specs,
        out_shape=out_shape,
        scratch_shapes=scratch,
        compiler_params=_params(2),
        name="seqmix_state" if has_state else "seqmix",
    )(*ins)
    return outs


def _merge_kernel(x_ref, obr_ref, wg_ref, bg_ref, wb_ref, wo_ref, lng_ref, lnb_ref, wq_ref, x1_ref, q_ref, *, br, d):
    x = x_ref[...]
    xb16 = x.astype(BF16)
    acc = None
    for g in range(N_BRANCH):
        cols = slice(9 * br + g * d, 9 * br + (g + 1) * d)
        gz = jnp.dot(xb16, wg_ref[:, cols], preferred_element_type=F32) + bg_ref[:, g * d:(g + 1) * d]
        proj = jnp.dot(obr_ref[:, g * br:(g + 1) * br], wb_ref[g], preferred_element_type=F32)
        term = _sigmoid(gz) * proj
        acc = term if acc is None else acc + term
    y = _bdot(acc, wo_ref[...])
    x1 = _layer_norm(DN_ALPHA * x + y, lng_ref[...], lnb_ref[...])
    x1_ref[...] = x1
    q_ref[...] = _bdot(x1, wq_ref[...]).astype(BF16)


def _merge(x, obr, p, tile):
    n, d = x.shape
    br = d // 2
    wts, layer = p["wts"], p["layer"]
    small = [p["b_in"][9 * br:].reshape(1, -1), p["ln_g"][0].reshape(1, d), p["ln_b"][0].reshape(1, d)]
    ins = [x, obr, wts["w_in"], small[0], wts["w_branch"], wts["w_out"], small[1], small[2], wts["x_wq"]]
    in_specs = [pl.BlockSpec((tile, d), lambda i: (i, 0)), pl.BlockSpec((tile, N_BRANCH * br), lambda i: (i, 0)),
                _layer_spec(wts["w_in"], layer), _resident(small[0].shape), _layer_spec(wts["w_branch"], layer),
                _layer_spec(wts["w_out"], layer), _resident(small[1].shape), _resident(small[2].shape),
                _layer_spec(wts["x_wq"], layer)]
    return pl.pallas_call(
        functools.partial(_merge_kernel, br=br, d=d),
        grid=(n // tile,),
        in_specs=in_specs,
        out_specs=[pl.BlockSpec((tile, d), lambda i: (i, 0))] * 2,
        out_shape=[jax.ShapeDtypeStruct((n, d), F32), jax.ShapeDtypeStruct((n, d), BF16)],
        compiler_params=_params(1),
        name="merge",
    )(*ins)


def _attn_kernel(q_ref, k_ref, v_ref, o_ref, *, hd):
    scale = hd ** -0.5
    outs = []
    for h in range(X_HEADS):
        sl = slice(h * hd, (h + 1) * hd)
        s = _bdot_nt(q_ref[:, sl], k_ref[:, sl]) * scale
        e = jnp.exp(s - jnp.max(s, axis=-1, keepdims=True))
        prob = e / jnp.sum(e, axis=-1, keepdims=True)
        outs.append(_bdot(prob, v_ref[:, sl]))
    o_ref[...] = jnp.concatenate(outs, axis=1).astype(BF16)


def _attn(q, mem_k, mem_v, layer, *, batch, seq, tile):
    n, d = q.shape
    n_t = seq // tile
    m = mem_k.shape[2]
    kv_spec = pl.BlockSpec((None, None, m, d), lambda b, t: (layer, b, 0, 0))
    return pl.pallas_call(
        functools.partial(_attn_kernel, hd=d // X_HEADS),
        grid=(batch, n_t),
        in_specs=[pl.BlockSpec((tile, d), lambda b, t: (b * n_t + t, 0)), kv_spec, kv_spec],
        out_specs=pl.BlockSpec((tile, d), lambda b, t: (b * n_t + t, 0)),
        out_shape=jax.ShapeDtypeStruct((n, d), BF16),
        compiler_params=_params(2),
        name="attn",
    )(q, mem_k, mem_v)


def _attn_cache_kernel(q_ref, k_hbm, v_hbm, o_ref, kbuf, vbuf, sem, *, layer, hd, rows, seq):
    g = pl.program_id(0)
    n_g = pl.num_programs(0)
    slot = lax.rem(g, 2)

    def head_copies(step, buf_slot):
        out = []
        src_rows = pl.ds(step * rows, rows)
        for h in range(X_HEADS):
            lanes = pl.ds(h * hd, hd)
            out.append(pltpu.make_async_copy(k_hbm.at[layer, src_rows, :, h, :], kbuf.at[buf_slot, :, :, lanes],
                                             sem.at[0, buf_slot, h]))
            out.append(pltpu.make_async_copy(v_hbm.at[layer, src_rows, :, h, :], vbuf.at[buf_slot, :, :, lanes],
                                             sem.at[1, buf_slot, h]))
        return out

    @pl.when(g == 0)
    def _first():
        for cp in head_copies(0, 0):
            cp.start()

    @pl.when(g + 1 < n_g)
    def _prefetch():
        for cp in head_copies(g + 1, 1 - slot):
            cp.start()

    for cp in head_copies(g, slot):
        cp.wait()

    scale = hd ** -0.5
    for r in range(rows):
        outs = []
        for h in range(X_HEADS):
            sl = slice(h * hd, (h + 1) * hd)
            s = _bdot_nt(q_ref[r * seq:(r + 1) * seq, sl], kbuf[slot, r, :, sl]) * scale
            e = jnp.exp(s - jnp.max(s, axis=-1, keepdims=True))
            prob = e / jnp.sum(e, axis=-1, keepdims=True)
            outs.append(_bdot(prob, vbuf[slot, r, :, sl]))
        o_ref[r * seq:(r + 1) * seq, :] = jnp.concatenate(outs, axis=1).astype(BF16)


def _attn_cache(q, cache_k, cache_v, layer, *, batch, seq, rows):
    n, d = q.shape
    m, hd = cache_k.shape[2], cache_k.shape[4]
    return pl.pallas_call(
        functools.partial(_attn_cache_kernel, layer=layer, hd=hd, rows=rows, seq=seq),
        grid=(batch // rows,),
        in_specs=[pl.BlockSpec((rows * seq, d), lambda g: (g, 0)),
                  pl.BlockSpec(memory_space=pl.ANY), pl.BlockSpec(memory_space=pl.ANY)],
        out_specs=pl.BlockSpec((rows * seq, d), lambda g: (g, 0)),
        out_shape=jax.ShapeDtypeStruct((n, d), BF16),
        scratch_shapes=[pltpu.VMEM((2, rows, m, d), F32), pltpu.VMEM((2, rows, m, d), F32),
                        pltpu.SemaphoreType.DMA((2, 2, X_HEADS))],
        compiler_params=_params(1),
        name="attn_cache",
    )(q, cache_k, cache_v)


def _route(s, sel):
    neg = -jnp.inf

    def top2(vals):
        m1 = functools.reduce(jnp.maximum, vals)
        i1 = jnp.full(m1.shape, len(vals) - 1, jnp.int32)
        for j in range(len(vals) - 2, -1, -1):
            i1 = jnp.where(vals[j] == m1, j, i1)
        rest = [jnp.where(i1 == j, neg, vals[j]) for j in range(len(vals))]
        m2 = functools.reduce(jnp.maximum, rest)
        i2 = jnp.full(m1.shape, len(vals) - 1, jnp.int32)
        for j in range(len(vals) - 2, -1, -1):
            i2 = jnp.where(rest[j] == m2, j, i2)
        return m1, m2, i1, i2

    groups = [top2(sel[g * EXPERTS_PER_GROUP:(g + 1) * EXPERTS_PER_GROUP]) for g in range(N_GROUPS)]
    score = [m1 + m2 for (m1, m2, _, _) in groups]
    best = functools.reduce(jnp.maximum, score)
    g_idx = jnp.full(best.shape, N_GROUPS - 1, jnp.int32)
    for g in range(N_GROUPS - 2, -1, -1):
        g_idx = jnp.where(score[g] == best, g, g_idx)
    e1 = jnp.zeros(best.shape, jnp.int32)
    e2 = jnp.zeros(best.shape, jnp.int32)
    for g in range(N_GROUPS):
        e1 = jnp.where(g_idx == g, g * EXPERTS_PER_GROUP + groups[g][2], e1)
        e2 = jnp.where(g_idx == g, g * EXPERTS_PER_GROUP + groups[g][3], e2)
    w1 = functools.reduce(jnp.add, [jnp.where(e1 == e, s[e], 0.0) for e in range(N_EXPERTS)])
    w2 = functools.reduce(jnp.add, [jnp.where(e2 == e, s[e], 0.0) for e in range(N_EXPERTS)])
    tot = w1 + w2
    return e1, e2, w1 / tot, w2 / tot


def _post_kernel(x1_ref, o_ref, wo_ref, lng_ref, lnb_ref, rwt_ref, rb_ref, tri_ref,
                 x2_ref, x2b_ref, rows_ref, cols_ref, seg_ref, *, tile):
    y = jnp.dot(o_ref[...], wo_ref[...], preferred_element_type=F32)
    x2 = _layer_norm(DN_ALPHA * x1_ref[...] + y, lng_ref[...], lnb_ref[...])
    x2_ref[...] = x2
    x2b_ref[...] = x2.astype(BF16)
    logits = lax.dot_general(rwt_ref[...], x2, (((1,), (1,)), ((), ())), preferred_element_type=F32,
                             precision=lax.Precision.HIGHEST)
    s_all = _sigmoid(logits)
    sel_all = s_all + rb_ref[...]
    s = [s_all[e:e + 1] for e in range(N_EXPERTS)]
    sel = [sel_all[e:e + 1] for e in range(N_EXPERTS)]
    e1, e2, w1, w2 = _route(s, sel)

    onehot = jnp.concatenate([jnp.where((e1 == e) | (e2 == e), 1.0, 0.0) for e in range(N_EXPERTS)], axis=0)
    prefix = jnp.dot(onehot.astype(BF16), tri_ref[...], preferred_element_type=F32)
    cnt = jnp.broadcast_to(jnp.sum(onehot, axis=-1, keepdims=True), (N_EXPERTS, LANES))
    cpad = jnp.floor((cnt + (SEG_ALIGN - 1.0)) * (1.0 / SEG_ALIGN)) * SEG_ALIGN
    ridx = lax.broadcasted_iota(jnp.int32, (N_EXPERTS, LANES), 0)
    inc = cpad
    k = 1
    while k < N_EXPERTS:
        inc = inc + jnp.where(ridx >= k, _shift_rows(inc, k), 0.0)
        k *= 2
    start = inc - cpad

    def pick(e_row, table):
        return functools.reduce(jnp.add, [jnp.where(e_row == e, table[e:e + 1], 0.0) for e in range(N_EXPERTS)])

    start_col = start[:, 0:1]
    pos1 = pick(e1, start_col) + pick(e1, prefix)
    pos2 = pick(e2, start_col) + pick(e2, prefix)
    rows = jnp.concatenate([pos1, pos2, w1, w2, jnp.zeros((SUBLANES - 4, tile), F32)], axis=0)
    rows_ref[...] = rows
    padded = jnp.concatenate([rows, jnp.zeros((LANES - SUBLANES, tile), F32)], axis=0)
    for j in range(tile // LANES):
        cols_ref[j * LANES:(j + 1) * LANES, :] = padded[:, j * LANES:(j + 1) * LANES].T
    seg_ref[0] = start
    seg_ref[1] = cnt


def _post(x1, o, p, router_wt, router_b, tri, tile):
    n, d = x1.shape
    wts, layer = p["wts"], p["layer"]
    small = [p["ln_g"][1].reshape(1, d), p["ln_b"][1].reshape(1, d), router_wt, router_b.reshape(N_EXPERTS, 1), tri]
    ins = [x1, o, wts["x_wo"]] + small
    in_specs = [pl.BlockSpec((tile, d), lambda i: (i, 0))] * 2 + [_layer_spec(wts["x_wo"], layer)]
    in_specs += [_resident(a.shape) for a in small]
    return pl.pallas_call(
        functools.partial(_post_kernel, tile=tile),
        grid=(n // tile,),
        in_specs=in_specs,
        out_specs=[pl.BlockSpec((tile, d), lambda i: (i, 0))] * 2 + [
            pl.BlockSpec((SUBLANES, tile), lambda i: (0, i)),
            pl.BlockSpec((tile, LANES), lambda i: (i, 0)),
            pl.BlockSpec((None, 2, N_EXPERTS, LANES), lambda i: (i, 0, 0, 0))],
        out_shape=[jax.ShapeDtypeStruct((n, d), F32), jax.ShapeDtypeStruct((n, d), BF16),
                   jax.ShapeDtypeStruct((SUBLANES, n), F32), jax.ShapeDtypeStruct((n, LANES), F32),
                   jax.ShapeDtypeStruct((n // tile, 2, N_EXPERTS, LANES), F32)],
        compiler_params=_params(1),
        name="post_attn_router",
    )(*ins)


def _moe_kernel(start_ref, cnt_ref, x2_ref, xb_ref, rows_ref, cols_ref, wg_ref, wu_ref, wd_ref, lng_ref, lnb_ref,
                out_ref, xs_ref, ys_ref, ws_ref, *, tile, rcap, rblk, tblk):
    i = pl.program_id(0)
    e = pl.program_id(1)

    @pl.when(e == 0)
    def _sort_rows():
        pos1, pos2 = rows_ref[0:1, :], rows_ref[1:2, :]
        w1, w2 = rows_ref[2:3, :], rows_ref[3:4, :]
        xb = xb_ref[...]
        for c in range(rcap // rblk):
            r = (c * rblk + lax.broadcasted_iota(jnp.int32, (rblk, tile), 0)).astype(F32)
            m1 = pos1 == r
            m2 = pos2 == r
            sel = jnp.where(m1 | m2, 1.0, 0.0).astype(BF16)
            xs_ref[c * rblk:(c + 1) * rblk, :] = jnp.dot(sel, xb, preferred_element_type=F32).astype(BF16)
            ws_ref[c * rblk:(c + 1) * rblk, :] = jnp.sum(jnp.where(m1, w1, 0.0) + jnp.where(m2, w2, 0.0),
                                                       axis=-1, keepdims=True)
        ys_ref[...] = jnp.zeros_like(ys_ref)

    start = start_ref[i * N_EXPERTS + e]
    n_chunks = cnt_ref[i * N_EXPERTS + e]

    def chunk_body(j, carry):
        r = pl.multiple_of(start + j * MOE_CHUNK, SEG_ALIGN)
        xs = xs_ref[pl.ds(r, MOE_CHUNK), :]
        gate = jnp.dot(xs, wg_ref[...], preferred_element_type=F32)
        up = jnp.dot(xs, wu_ref[...], preferred_element_type=F32)
        hid = gate * _sigmoid(gate) * up
        y = _bdot(hid, wd_ref[...])
        ys_ref[pl.ds(r, MOE_CHUNK), :] = (ws_ref[pl.ds(r, MOE_CHUNK), :] * y).astype(BF16)
        return carry

    lax.fori_loop(0, n_chunks, chunk_body, 0)

    @pl.when(e == pl.num_programs(1) - 1)
    def _finish():
        for c in range(tile // tblk):
            sl = slice(c * tblk, (c + 1) * tblk)
            r = lax.broadcasted_iota(jnp.int32, (tblk, rcap), 1).astype(F32)
            back = jnp.where((cols_ref[sl, 0:1] == r) | (cols_ref[sl, 1:2] == r), 1.0, 0.0).astype(BF16)
            y = jnp.dot(back, ys_ref[...], preferred_element_type=F32)
            out_ref[sl, :] = _layer_norm(DN_ALPHA * x2_ref[sl, :] + y, lng_ref[...], lnb_ref[...])


def _moe(x2, x2b, rows, cols, seg, p, tile):
    n, d = x2.shape
    wts, layer = p["wts"], p["layer"]
    de = wts["e_wg"].shape[-1]
    tblk = min(256, tile)
    rcap = -(-(2 * tile + N_EXPERTS * SEG_ALIGN + MOE_CHUNK) // SEG_ALIGN) * SEG_ALIGN
    rblk = max(b for b in range(SEG_ALIGN, 257, SEG_ALIGN) if rcap % b == 0)
    seg_start = seg[:, 0, :, 0].astype(jnp.int32).reshape(-1)
    seg_cnt = (seg[:, 1, :, 0].astype(jnp.int32).reshape(-1) + (MOE_CHUNK - 1)) // MOE_CHUNK
    row_spec = lambda w: pl.BlockSpec((tile, w), lambda i, e, s, c: (i, 0))
    grid_spec = pltpu.PrefetchScalarGridSpec(
        num_scalar_prefetch=2,
        grid=(n // tile, N_EXPERTS),
        in_specs=[row_spec(d), row_spec(d),
                  pl.BlockSpec((SUBLANES, tile), lambda i, e, s, c: (0, i)),
                  row_spec(LANES),
                  pl.BlockSpec((None, None, d, de), lambda i, e, s, c: (layer, e, 0, 0)),
                  pl.BlockSpec((None, None, d, de), lambda i, e, s, c: (layer, e, 0, 0)),
                  pl.BlockSpec((None, None, de, d), lambda i, e, s, c: (layer, e, 0, 0)),
                  pl.BlockSpec((1, d), lambda i, e, s, c: (0, 0)),
                  pl.BlockSpec((1, d), lambda i, e, s, c: (0, 0))],
        out_specs=row_spec(d),
        scratch_shapes=[pltpu.VMEM((rcap, d), BF16), pltpu.VMEM((rcap, d), BF16), pltpu.VMEM((rcap, 1), F32)],
    )
    return pl.pallas_call(
        functools.partial(_moe_kernel, tile=tile, rcap=rcap, rblk=rblk, tblk=tblk),
        grid_spec=grid_spec,
        out_shape=jax.ShapeDtypeStruct((n, d), F32),
        compiler_params=_params(2),
        name="moe",
    )(seg_start, seg_cnt, x2, x2b, rows, cols, wts["e_wg"], wts["e_wu"], wts["e_wd"],
      p["ln_g"][2].reshape(1, d), p["ln_b"][2].reshape(1, d))


def _pick(n, pref):
    return pref if n % pref == 0 else n


def kernel(x_prompt, x_sample, mem_prompt, cache_mem_k, cache_mem_v, state_conv, state_rglru, state_pool, state_hgrn, ln_in_g, ln_in_b, w_in, b_in, sgu_ln_g, sgu_ln_b, sgu_w, sgu_b, conv_w, conv_b, lru_wr, lru_br, lru_wi, lru_bi, lru_lambda, pool_w, pool_b, pool_scale, hgrn_lb, hgrn_norm_g, w_branch, w_out, x_wq, x_wk, x_wv, x_wo, ln_g, ln_b, router_w, router_b, e_wg, e_wu, e_wd):
    bp, tp, d = x_prompt.shape
    bs, ts, _ = x_sample.shape
    depth = w_in.shape[0]
    br = d // 2
    n_mem = mem_prompt.shape[1]
    hd = br // HG_HEADS
    mixw = 9 * br
    assert depth == MODEL_DEPTH

    lb_cum = jnp.cumsum(jax.nn.softmax(hgrn_lb.astype(F32), axis=0), axis=0)
    lbs = lb_cum - lb_cum[0]

    wts = dict(w_in=w_in, w_branch=w_branch, w_out=w_out, x_wq=x_wq, x_wo=x_wo, lru_wr=lru_wr, lru_wi=lru_wi,
               pool_w=pool_w, e_wg=e_wg, e_wu=e_wu, e_wd=e_wd)
    wts = {k: v.astype(BF16) for k, v in wts.items()}
    router_wt = router_w.T

    tile_p = SGU_CHUNK if tp % SGU_CHUNK == 0 else tp
    tile_s = SGU_CHUNK if ts % SGU_CHUNK == 0 else ts
    sub_p = _pick(tile_p, SUBLANES)
    sub_s = _pick(tile_s, SUBLANES)

    n_p, n_s = bp * tp, bs * ts
    xp = _input_ln(x_prompt.reshape(n_p, d), ln_in_g, ln_in_b, _pick(n_p, 512))
    xs = _input_ln(x_sample.reshape(n_s, d), ln_in_g, ln_in_b, _pick(n_s, 512))

    mem_flat = mem_prompt.reshape(bp * n_mem, d)
    mk_all, mv_all, mkb, mvb = _memory_kv(mem_flat, x_wk.astype(BF16), x_wv.astype(BF16), _pick(bp * n_mem, 512))
    mk_all = mk_all.reshape(depth, bp, n_mem, X_HEADS, d // X_HEADS)
    mv_all = mv_all.reshape(depth, bp, n_mem, X_HEADS, d // X_HEADS)
    mkb = mkb.reshape(depth, bp, n_mem, d)
    mvb = mvb.reshape(depth, bp, n_mem, d)

    moe_tile_p = _pick(n_p, MOE_TILE)
    moe_tile_s = _pick(n_s, MOE_TILE)

    def strict_upper(t):
        idx = jnp.arange(t)
        return (idx[:, None] < idx[None, :]).astype(BF16)

    tri_p, tri_s = strict_upper(moe_tile_p), strict_upper(moe_tile_s)

    outs_p, outs_s = [], []
    for l in range(depth):
        p = dict(wts=wts, layer=l, b_in=b_in[l], sgu_ln_g=sgu_ln_g[l],
                 sgu_ln_b=sgu_ln_b[l], sgu_w=sgu_w[l], sgu_b=sgu_b[l], conv_w=conv_w[l], conv_b=conv_b[l],
                 lru_br=lru_br[l], lru_bi=lru_bi[l], lru_lambda=lru_lambda[l], pool_b=pool_b[l],
                 pool_scale=pool_scale[l], hgrn_norm_g=hgrn_norm_g[l], ln_g=ln_g[l], ln_b=ln_b[l])

        obr_p, conv_p, h_p, pool_p, s_p = _seqmix(xp, p, lbs[l], None, batch=bp, seq=tp, tile=tile_p, sub=sub_p,
                                                   pos0=0, emit_v=False, projected=False)
        zs = _project(xs, p, _pick(n_s, 256))
        obr_s, conv_s, h_s, pool_s, s_s, v_s = _seqmix(
            zs, p, lbs[l], (state_conv[l], state_rglru[l], state_pool[l], state_hgrn[l]),
            batch=bs, seq=ts, tile=tile_s, sub=sub_s, pos0=PAST_LEN, emit_v=True, projected=True)
        outs_p.append((conv_p, h_p.reshape(bp, br), pool_p, s_p))
        outs_s.append((conv_s, h_s.reshape(bs, br), pool_s, s_s, v_s.reshape(bs, ts, br)))

        x1p, qp = _merge(xp, obr_p, p, _pick(n_p, 256))
        x1s, qs = _merge(xs, obr_s, p, _pick(n_s, 256))
        op = _attn(qp, mkb, mvb, l, batch=bp, seq=tp, tile=_pick(tp, 256))
        os_ = _attn_cache(qs, cache_mem_k, cache_mem_v, l, batch=bs, seq=ts, rows=_pick(bs, CACHE_ROWS))
        x2p, x2pb, rows_p, cols_p, seg_p = _post(x1p, op, p, router_wt, router_b, tri_p, moe_tile_p)
        x2s, x2sb, rows_s, cols_s, seg_s = _post(x1s, os_, p, router_wt, router_b, tri_s, moe_tile_s)
        xp = _moe(x2p, x2pb, rows_p, cols_p, seg_p, p, moe_tile_p)
        xs = _moe(x2s, x2sb, rows_s, cols_s, seg_s, p, moe_tile_s)

    stack = lambda outs, i: jnp.stack([o[i] for o in outs])
    return (xp.reshape(bp, tp, d), xs.reshape(bs, ts, d), mk_all, mv_all,
            stack(outs_p, 0), stack(outs_p, 1), stack(outs_p, 2), stack(outs_p, 3),
            stack(outs_s, 0), stack(outs_s, 1), stack(outs_s, 2), stack(outs_s, 3), stack(outs_s, 4))
```

```python
import functools

import jax
import jax.numpy as jnp
from jax import lax
from jax.experimental import pallas as pl
from jax.experimental.pallas import tpu as pltpu

F32 = jnp.float32
BF16 = jnp.bfloat16

N_BRANCH = 4
SGU_GROUPS = 4
SGU_CHUNK = 128
LRU_HEADS = 4
CONV_W = 4
LRU_C = 8.0
POOL_WINDOWS = (2, 4, 8, 16)
POOL_BUF = max(POOL_WINDOWS) - 1
HG_HEADS = 4
X_HEADS = 4
N_EXPERTS = 16
N_GROUPS = 4
EXPERTS_PER_GROUP = N_EXPERTS // N_GROUPS
MODEL_DEPTH = 4
DN_ALPHA = (2 * MODEL_DEPTH) ** 0.25
LN_EPS = 1e-5
RMS_EPS = 1e-6
PAST_LEN = 16384
MOE_TILE = 1024
MOE_EXPERTS_PER_STEP = 2
MOE_CHUNK = 160
SEG_ALIGN = 16
CACHE_ROWS = 4
HG_ROW_BLOCK = 64

V7X_VMEM_BYTES = 64 * 1024 * 1024
VMEM_LIMIT = V7X_VMEM_BYTES - 8 * 1024 * 1024
SUBLANES = 8
LANES = 128


def _params(n_grid):
    return pltpu.CompilerParams(dimension_semantics=("arbitrary",) * n_grid, vmem_limit_bytes=VMEM_LIMIT)


def _resident(shape):
    nd = len(shape)
    return pl.BlockSpec(shape, lambda *_: (0,) * nd, pipeline_mode=pl.Buffered(1))


def _layer_spec(arr, layer):
    nd = arr.ndim - 1
    return pl.BlockSpec((None,) + tuple(arr.shape[1:]), lambda *_: (layer,) + (0,) * nd, pipeline_mode=pl.Buffered(1))


def _layer_norm(x, g, b):
    mu = jnp.mean(x, axis=-1, keepdims=True)
    xc = x - mu
    var = jnp.mean(xc * xc, axis=-1, keepdims=True)
    return xc * lax.rsqrt(var + LN_EPS) * g + b


def _sigmoid(x):
    return 1.0 / (1.0 + jnp.exp(-x))


def _bdot(a, b):
    return jnp.dot(a.astype(BF16), b.astype(BF16), preferred_element_type=F32)


def _bdot_nt(a, b):
    return lax.dot_general(a.astype(BF16), b.astype(BF16), (((1,), (1,)), ((), ())), preferred_element_type=F32)


def _bdot_tn(a, b):
    return lax.dot_general(a.astype(BF16), b.astype(BF16), (((0,), (0,)), ((), ())), preferred_element_type=F32)


def _shift_rows(x, k):
    return pltpu.roll(x, k, 0)


def _shift_rows_fill(x, k, fill):
    n, c = x.shape
    if k % SUBLANES == 0:
        return jnp.concatenate([jnp.full((k, c), fill, x.dtype), x[:n - k]], axis=0)
    row = lax.broadcasted_iota(jnp.int32, x.shape, 0)
    return jnp.where(row >= k, pltpu.roll(x, k, 0), fill)


def _rotate_in_groups(x, k):
    n, c = x.shape
    return pltpu.roll(x.reshape(n // SUBLANES, SUBLANES, c), k, 1).reshape(n, c)


def _ln_kernel(x_ref, g_ref, b_ref, o_ref):
    o_ref[...] = _layer_norm(x_ref[...], g_ref[...], b_ref[...])


def _input_ln(x, g, b, tile):
    n, d = x.shape
    return pl.pallas_call(
        _ln_kernel,
        grid=(n // tile,),
        in_specs=[pl.BlockSpec((tile, d), lambda i: (i, 0)), _resident((1, d)), _resident((1, d))],
        out_specs=pl.BlockSpec((tile, d), lambda i: (i, 0)),
        out_shape=jax.ShapeDtypeStruct((n, d), F32),
        compiler_params=_params(1),
        name="input_ln",
    )(x, g.reshape(1, d), b.reshape(1, d))


def _kv_kernel(m_ref, wk_ref, wv_ref, k_ref, v_ref, kb_ref, vb_ref, *, hd):
    m = m_ref[...].astype(BF16)
    k = jnp.dot(m, wk_ref[...], preferred_element_type=F32)
    v = jnp.dot(m, wv_ref[...], preferred_element_type=F32)
    for h in range(X_HEADS):
        k_ref[:, h, :] = k[:, h * hd:(h + 1) * hd]
        v_ref[:, h, :] = v[:, h * hd:(h + 1) * hd]
    kb_ref[...] = k.astype(BF16)
    vb_ref[...] = v.astype(BF16)


def _memory_kv(mem, wk, wv, tile):
    n, d = mem.shape
    depth = wk.shape[0]
    hd = d // X_HEADS
    out = jax.ShapeDtypeStruct((depth, n, X_HEADS, hd), F32)
    outb = jax.ShapeDtypeStruct((depth, n, d), BF16)
    return pl.pallas_call(
        functools.partial(_kv_kernel, hd=hd),
        grid=(depth, n // tile),
        in_specs=[
            pl.BlockSpec((tile, d), lambda l, i: (i, 0)),
            pl.BlockSpec((None, d, d), lambda l, i: (l, 0, 0)),
            pl.BlockSpec((None, d, d), lambda l, i: (l, 0, 0)),
        ],
        out_specs=[pl.BlockSpec((None, tile, X_HEADS, hd), lambda l, i: (l, i, 0, 0))] * 2
        + [pl.BlockSpec((None, tile, d), lambda l, i: (l, i, 0))] * 2,
        out_shape=[out, out, outb, outb],
        compiler_params=_params(2),
        name="memory_kv",
    )(mem, wk, wv)


def _project_kernel(x_ref, w_ref, b_ref, z_ref, *, width):
    z_ref[...] = jnp.dot(x_ref[...].astype(BF16), w_ref[:, :width], preferred_element_type=F32) + b_ref[...]


def _project(x, p, tile):
    n, d = x.shape
    wts, layer = p["wts"], p["layer"]
    width = 9 * (d // 2)
    return pl.pallas_call(
        functools.partial(_project_kernel, width=width),
        grid=(n // tile,),
        in_specs=[pl.BlockSpec((tile, d), lambda i: (i, 0)), _layer_spec(wts["w_in"], layer), _resident((1, width))],
        out_specs=pl.BlockSpec((tile, width), lambda i: (i, 0)),
        out_shape=jax.ShapeDtypeStruct((n, width), F32),
        compiler_params=_params(1),
        name="project",
    )(x, wts["w_in"], p["b_in"][:width].reshape(1, width))


def _seqmix_kernel(*refs, tile, sub, pos0, has_state, emit_v, br, projected):
    if projected:
        z_ref = refs[0]
        refs = refs[1:]
    else:
        x_ref, win_ref, bin_ref = refs[:3]
        refs = refs[3:]
    (sgug_ref, sgub_ref, sguw_ref, sgubt_ref, convw_ref, convb_ref, wr_ref, br_ref,
     wi_ref, bi_ref, lam_ref, poolw_ref, poolb_ref, pools_ref, lb_ref, hgn_ref) = refs[:16]
    refs = refs[16:]
    if has_state:
        conv0_ref, h0_ref, pool0_ref, s0_ref = refs[:4]
        refs = refs[4:]
    obr_ref, convn_ref, hn_ref, pooln_ref, sn_ref = refs[:5]
    refs = refs[5:]
    if emit_v:
        vrows_ref = refs[0]
        refs = refs[1:]
    conv_sc, pool_sc, h_sc, st_sc = refs

    t = pl.program_id(1)
    n_t = pl.num_programs(1)
    gd = br // SGU_GROUPS
    hd = br // HG_HEADS

    @pl.when(t == 0)
    def _init():
        conv_sc[...] = jnp.zeros_like(conv_sc)
        pool_sc[...] = jnp.zeros_like(pool_sc)
        if has_state:
            conv_sc[SUBLANES - (CONV_W - 1):, :] = conv0_ref[...]
            pool_sc[1:, :] = pool0_ref[...]
            h_sc[...] = h0_ref[...]
            for h in range(HG_HEADS):
                st_sc[h] = s0_ref[h].T
        else:
            h_sc[...] = jnp.zeros_like(h_sc)
            st_sc[...] = jnp.zeros_like(st_sc)

    if projected:
        z = z_ref[...]
    else:
        z = jnp.dot(x_ref[...].astype(BF16), win_ref[:, :9 * br], preferred_element_type=F32) + bin_ref[...]
    u, v, yg, xb, c, q, f, vi, go = [z[:, k * br:(k + 1) * br] for k in range(9)]
    row = lax.broadcasted_iota(jnp.int32, (tile, br), 0)

    vn = _layer_norm(v, sgug_ref[...], sgub_ref[...])
    if emit_v:
        vrows_ref[...] = vn
    rr = lax.broadcasted_iota(jnp.int32, (tile, tile), 0)
    cc = lax.broadcasted_iota(jnp.int32, (tile, tile), 1)
    o_a = []
    for g in range(SGU_GROUPS):
        w = jnp.where(rr >= cc, sguw_ref[g, :tile, :tile], 0.0)
        mixed = _bdot(w, vn[:, g * gd:(g + 1) * gd]) + sgubt_ref[:tile, g:g + 1]
        o_a.append(u[:, g * gd:(g + 1) * gd] * mixed)
    o_a = jnp.concatenate(o_a, axis=1)

    ext = jnp.concatenate([conv_sc[...], xb], axis=0)
    cw = convw_ref[...]
    base = SUBLANES - (CONV_W - 1)
    xc = convb_ref[...] + xb * cw[CONV_W - 1:CONV_W]
    for j in range(CONV_W - 1):
        xc = xc + ext[base + j:base + j + tile] * cw[j:j + 1]
    conv_sc[...] = ext[tile:tile + SUBLANES]
    convn_ref[...] = ext[tile + base:tile + SUBLANES]

    def heads_dot(a, w_ref):
        return jnp.concatenate([_bdot(a[:, h * hd:(h + 1) * hd], w_ref[h]) for h in range(LRU_HEADS)], axis=1)

    r_g = _sigmoid(heads_dot(xc, wr_ref) + br_ref[...])
    i_g = _sigmoid(heads_dot(xc, wi_ref) + bi_ref[...])
    lam = lam_ref[...]
    softplus_neg_lam = jnp.maximum(-lam, 0.0) + jnp.log1p(jnp.exp(-jnp.abs(lam)))
    log_a = (-LRU_C) * r_g * softplus_neg_lam
    a = jnp.exp(log_a)
    y2 = 2.0 * log_a
    one_minus = jnp.tanh(-0.5 * y2) * (jnp.exp(y2) + 1.0)
    bx = jnp.sqrt(one_minus) * (i_g * xc)
    h_prev = h_sc[...]
    hseq = []
    for blk in range(br // LANES):
        ls = slice(blk * LANES, (blk + 1) * LANES)
        sa, sb = a[:, ls], bx[:, ls]
        k = 1
        while k < tile:
            sb = sa * _shift_rows_fill(sb, k, 0.0) + sb
            sa = sa * _shift_rows_fill(sa, k, 1.0)
            k *= 2
        hseq.append(sb + sa * h_prev[:, ls])
    hseq = jnp.concatenate(hseq, axis=1)
    h_sc[...] = hseq[tile - 1:tile]
    hn_ref[...] = hseq[tile - 1:tile]
    gelu = 0.5 * yg * (1.0 + jnp.tanh(0.7978845608028654 * (yg + 0.044715 * (yg * yg * yg))))
    o_b = gelu * hseq

    extp = jnp.concatenate([pool_sc[...], c], axis=0)
    pb = POOL_BUF + 1
    tpos = pos0 + 1 + t * tile + lax.broadcasted_iota(jnp.int32, (tile, 1), 0)
    o_c = []
    for g, wdw in enumerate(POOL_WINDOWS):
        wsum = extp[:, g * gd:(g + 1) * gd]
        wdt = 1
        while wdt < wdw:
            wsum = wsum + (_shift_rows_fill(wsum, wdt, 0.0) if wdt % SUBLANES == 0 else _shift_rows(wsum, wdt))
            wdt *= 2
        cnt = jnp.minimum(wdw, tpos).astype(F32)
        pooled = wsum[pb:pb + tile] / cnt
        o_c.append(_bdot(pooled - c[:, g * gd:(g + 1) * gd], poolw_ref[g]))
    o_c = (jnp.concatenate(o_c, axis=1) + poolb_ref[...]) * pools_ref[...]
    pool_sc[...] = extp[tile:tile + pb]
    pooln_ref[...] = extp[tile + 1:tile + pb]

    lb = lb_ref[...]
    log_sig = jnp.minimum(f, 0.0) - jnp.log1p(jnp.exp(-jnp.abs(f)))
    a1 = jnp.log(lb)
    a2 = jnp.log1p(-lb) + log_sig
    log_f = jnp.maximum(a1, a2) + jnp.log1p(jnp.exp(-jnp.abs(a1 - a2)))
    kf = (1.0 - lb) * _sigmoid(-f)
    qf = q * _sigmoid(q)
    shift = _rotate_in_groups if sub == SUBLANES else _shift_rows
    rows_blk = min(tile, HG_ROW_BLOCK)
    rmod = lax.broadcasted_iota(jnp.int32, (rows_blk, hd), 0) % sub
    bc, o_d = [], []
    for h in range(HG_HEADS):
        ls = slice(h * hd, (h + 1) * hd)
        bc_h, o_h = [], []
        for r0 in range(0, tile, rows_blk):
            rs = slice(r0, r0 + rows_blk)
            qb, kb, vb, bb = qf[rs, ls], kf[rs, ls], vi[rs, ls], log_f[rs, ls]
            k = 1
            while k < sub:
                bb = bb + jnp.where(rmod >= k, shift(bb, k), 0.0)
                k *= 2
            acc = jnp.sum(qb * kb, axis=-1, keepdims=True) * vb
            for dd in range(1, sub):
                diff = jnp.where(rmod >= dd, bb - shift(bb, dd), -jnp.inf)
                att = jnp.sum(qb * shift(kb, dd) * jnp.exp(diff), axis=-1, keepdims=True)
                acc = acc + att * shift(vb, dd)
            bc_h.append(bb)
            o_h.append(acc)
        bc.append(jnp.concatenate(bc_h, axis=0) if len(bc_h) > 1 else bc_h[0])
        o_d.append(jnp.concatenate(o_h, axis=0) if len(o_h) > 1 else o_h[0])
    bc = jnp.concatenate(bc, axis=1)
    qe = qf * jnp.exp(bc)
    for j in range(tile // sub):
        lo, hi = j * sub, (j + 1) * sub
        b_last = bc[hi - 1:hi]
        kdec = kf[lo:hi] * jnp.exp(b_last - bc[lo:hi])
        decay = jnp.exp(b_last)
        inter = []
        for h in range(HG_HEADS):
            st = st_sc[h]
            inter.append(_bdot_nt(qe[lo:hi, h * hd:(h + 1) * hd], st))
            upd = _bdot_tn(vi[lo:hi, h * hd:(h + 1) * hd], kdec[:, h * hd:(h + 1) * hd])
            st_sc[h] = st * decay[:, h * hd:(h + 1) * hd] + upd
        inter = jnp.concatenate(inter, axis=1)
        if j == 0:
            o_inter = [inter]
        else:
            o_inter.append(inter)
    o_inter = jnp.concatenate(o_inter, axis=0) if len(o_inter) > 1 else o_inter[0]
    gn = hgn_ref[...]
    o_n = []
    for h in range(HG_HEADS):
        oh = o_d[h] + o_inter[:, h * hd:(h + 1) * hd]
        ms = jnp.mean(oh * oh, axis=-1, keepdims=True)
        o_n.append(oh * lax.rsqrt(ms + RMS_EPS) * gn)
    o_d = jnp.concatenate(o_n, axis=1) * (go * _sigmoid(go))

    obr_ref[...] = jnp.concatenate([o_a, o_b, o_c, o_d], axis=1).astype(BF16)

    @pl.when(t == n_t - 1)
    def _emit_state():
        for h in range(HG_HEADS):
            sn_ref[h] = st_sc[h].T


def _seqmix(x, p, lb, states, *, batch, seq, tile, sub, pos0, emit_v, projected):
    n = x.shape[0]
    wts, layer = p["wts"], p["layer"]
    d = wts["w_in"].shape[1]
    br = d // 2
    hd = br // HG_HEADS
    n_t = seq // tile
    has_state = states is not None
    mixw = 9 * br
    row = lambda a: a.reshape(1, -1)
    small = [row(p["sgu_ln_g"]), row(p["sgu_ln_b"]), p["sgu_w"], p["sgu_b"].T, p["conv_w"], row(p["conv_b"])]
    ins = [x]
    in_specs = [pl.BlockSpec((tile, x.shape[1]), lambda b, t: (b * n_t + t, 0))]
    if not projected:
        ins += [wts["w_in"], row(p["b_in"][:mixw])]
        in_specs += [_layer_spec(wts["w_in"], layer), _resident((1, mixw))]
    tail = [None, row(p["lru_br"]), None, row(p["lru_bi"]), row(p["lru_lambda"]), None, row(p["pool_b"]),
            row(p["pool_scale"]), row(lb), row(p["hgrn_norm_g"])]
    stacked = {0: "lru_wr", 2: "lru_wi", 5: "pool_w"}
    ins += small
    in_specs += [_resident(a.shape) for a in small]
    for i, a in enumerate(tail):
        if a is None:
            ins.append(wts[stacked[i]])
            in_specs.append(_layer_spec(wts[stacked[i]], layer))
        else:
            ins.append(a)
            in_specs.append(_resident(a.shape))
    if has_state:
        conv0, h0, pool0, s0 = states
        ins += [conv0, h0.reshape(batch, 1, br), pool0, s0]
        in_specs += [
            pl.BlockSpec((None, CONV_W - 1, br), lambda b, t: (b, 0, 0)),
            pl.BlockSpec((None, 1, br), lambda b, t: (b, 0, 0)),
            pl.BlockSpec((None, POOL_BUF, br), lambda b, t: (b, 0, 0)),
            pl.BlockSpec((None, HG_HEADS, hd, hd), lambda b, t: (b, 0, 0, 0)),
        ]
    out_shape = [
        jax.ShapeDtypeStruct((n, N_BRANCH * br), BF16),
        jax.ShapeDtypeStruct((batch, CONV_W - 1, br), F32),
        jax.ShapeDtypeStruct((batch, 1, br), F32),
        jax.ShapeDtypeStruct((batch, POOL_BUF, br), F32),
        jax.ShapeDtypeStruct((batch, HG_HEADS, hd, hd), F32),
    ]
    out_specs = [
        pl.BlockSpec((tile, N_BRANCH * br), lambda b, t: (b * n_t + t, 0)),
        pl.BlockSpec((None, CONV_W - 1, br), lambda b, t: (b, 0, 0)),
        pl.BlockSpec((None, 1, br), lambda b, t: (b, 0, 0)),
        pl.BlockSpec((None, POOL_BUF, br), lambda b, t: (b, 0, 0)),
        pl.BlockSpec((None, HG_HEADS, hd, hd), lambda b, t: (b, 0, 0, 0)),
    ]
    if emit_v:
        out_shape.append(jax.ShapeDtypeStruct((n, br), F32))
        out_specs.append(pl.BlockSpec((tile, br), lambda b, t: (b * n_t + t, 0)))
    scratch = [
        pltpu.VMEM((SUBLANES, br), F32),
        pltpu.VMEM((POOL_BUF + 1, br), F32),
        pltpu.VMEM((1, br), F32),
        pltpu.VMEM((HG_HEADS, hd, hd), F32),
    ]
    kern = functools.partial(_seqmix_kernel, tile=tile, sub=sub, pos0=pos0, has_state=has_state, emit_v=emit_v, br=br,
                             projected=projected)
    outs = pl.pallas_call(
        kern,
        grid=(batch, n_t),
        in_specs=in_specs,
        out_specs=out_specs,
        out_shape=out_shape,
        scratch_shapes=scratch,
        compiler_params=_params(2),
        name="seqmix_state" if has_state else "seqmix",
    )(*ins)
    return outs


def _merge_kernel(x_ref, obr_ref, wg_ref, bg_ref, wb_ref, wo_ref, lng_ref, lnb_ref, wq_ref, x1_ref, q_ref, *, br, d):
    x = x_ref[...]
    xb16 = x.astype(BF16)
    acc = None
    for g in range(N_BRANCH):
        cols = slice(9 * br + g * d, 9 * br + (g + 1) * d)
        gz = jnp.dot(xb16, wg_ref[:, cols], preferred_element_type=F32) + bg_ref[:, g * d:(g + 1) * d]
        proj = jnp.dot(obr_ref[:, g * br:(g + 1) * br], wb_ref[g], preferred_element_type=F32)
        term = _sigmoid(gz) * proj
        acc = term if acc is None else acc + term
    y = _bdot(acc, wo_ref[...])
    x1 = _layer_norm(DN_ALPHA * x + y, lng_ref[...], lnb_ref[...])
    x1_ref[...] = x1
    q_ref[...] = _bdot(x1, wq_ref[...]).astype(BF16)


def _merge(x, obr, p, tile):
    n, d = x.shape
    br = d // 2
    wts, layer = p["wts"], p["layer"]
    small = [p["b_in"][9 * br:].reshape(1, -1), p["ln_g"][0].reshape(1, d), p["ln_b"][0].reshape(1, d)]
    ins = [x, obr, wts["w_in"], small[0], wts["w_branch"], wts["w_out"], small[1], small[2], wts["x_wq"]]
    in_specs = [pl.BlockSpec((tile, d), lambda i: (i, 0)), pl.BlockSpec((tile, N_BRANCH * br), lambda i: (i, 0)),
                _layer_spec(wts["w_in"], layer), _resident(small[0].shape), _layer_spec(wts["w_branch"], layer),
                _layer_spec(wts["w_out"], layer), _resident(small[1].shape), _resident(small[2].shape),
                _layer_spec(wts["x_wq"], layer)]
    return pl.pallas_call(
        functools.partial(_merge_kernel, br=br, d=d),
        grid=(n // tile,),
        in_specs=in_specs,
        out_specs=[pl.BlockSpec((tile, d), lambda i: (i, 0))] * 2,
        out_shape=[jax.ShapeDtypeStruct((n, d), F32), jax.ShapeDtypeStruct((n, d), BF16)],
        compiler_params=_params(1),
        name="merge",
    )(*ins)


def _attn_kernel(q_ref, k_ref, v_ref, o_ref, *, hd):
    scale = hd ** -0.5
    outs = []
    for h in range(X_HEADS):
        sl = slice(h * hd, (h + 1) * hd)
        s = _bdot_nt(q_ref[:, sl], k_ref[:, sl]) * scale
        e = jnp.exp(s - jnp.max(s, axis=-1, keepdims=True))
        prob = e / jnp.sum(e, axis=-1, keepdims=True)
        outs.append(_bdot(prob, v_ref[:, sl]))
    o_ref[...] = jnp.concatenate(outs, axis=1).astype(BF16)


def _attn(q, mem_k, mem_v, layer, *, batch, seq, tile):
    n, d = q.shape
    n_t = seq // tile
    m = mem_k.shape[2]
    kv_spec = pl.BlockSpec((None, None, m, d), lambda b, t: (layer, b, 0, 0))
    return pl.pallas_call(
        functools.partial(_attn_kernel, hd=d // X_HEADS),
        grid=(batch, n_t),
        in_specs=[pl.BlockSpec((tile, d), lambda b, t: (b * n_t + t, 0)), kv_spec, kv_spec],
        out_specs=pl.BlockSpec((tile, d), lambda b, t: (b * n_t + t, 0)),
        out_shape=jax.ShapeDtypeStruct((n, d), BF16),
        compiler_params=_params(2),
        name="attn",
    )(q, mem_k, mem_v)


def _attn_cache_kernel(q_ref, k_hbm, v_hbm, o_ref, kbuf, vbuf, sem, *, layer, hd, rows, seq):
    g = pl.program_id(0)
    n_g = pl.num_programs(0)
    slot = lax.rem(g, 2)

    def head_copies(step, buf_slot):
        out = []
        src_rows = pl.ds(step * rows, rows)
        for h in range(X_HEADS):
            lanes = pl.ds(h * hd, hd)
            out.append(pltpu.make_async_copy(k_hbm.at[layer, src_rows, :, h, :], kbuf.at[buf_slot, :, :, lanes],
                                             sem.at[0, buf_slot, h]))
            out.append(pltpu.make_async_copy(v_hbm.at[layer, src_rows, :, h, :], vbuf.at[buf_slot, :, :, lanes],
                                             sem.at[1, buf_slot, h]))
        return out

    @pl.when(g == 0)
    def _first():
        for cp in head_copies(0, 0):
            cp.start()

    @pl.when(g + 1 < n_g)
    def _prefetch():
        for cp in head_copies(g + 1, 1 - slot):
            cp.start()

    for cp in head_copies(g, slot):
        cp.wait()

    pairs = [(r, h) for r in range(rows) for h in range(X_HEADS)]
    s = jnp.concatenate([_bdot_nt(q_ref[r * seq:(r + 1) * seq, h * hd:(h + 1) * hd], kbuf[slot, r, :, h * hd:(h + 1) * hd])
                         for r, h in pairs], axis=0) * (hd ** -0.5)
    e = jnp.exp(s - jnp.max(s, axis=-1, keepdims=True))
    prob = e / jnp.sum(e, axis=-1, keepdims=True)
    outs = [_bdot(prob[i * seq:(i + 1) * seq], vbuf[slot, r, :, h * hd:(h + 1) * hd]) for i, (r, h) in enumerate(pairs)]
    for r in range(rows):
        o_ref[r * seq:(r + 1) * seq, :] = jnp.concatenate(outs[r * X_HEADS:(r + 1) * X_HEADS], axis=1).astype(BF16)


def _attn_cache(q, cache_k, cache_v, layer, *, batch, seq, rows):
    n, d = q.shape
    m, hd = cache_k.shape[2], cache_k.shape[4]
    return pl.pallas_call(
        functools.partial(_attn_cache_kernel, layer=layer, hd=hd, rows=rows, seq=seq),
        grid=(batch // rows,),
        in_specs=[pl.BlockSpec((rows * seq, d), lambda g: (g, 0)),
                  pl.BlockSpec(memory_space=pl.ANY), pl.BlockSpec(memory_space=pl.ANY)],
        out_specs=pl.BlockSpec((rows * seq, d), lambda g: (g, 0)),
        out_shape=jax.ShapeDtypeStruct((n, d), BF16),
        scratch_shapes=[pltpu.VMEM((2, rows, m, d), F32), pltpu.VMEM((2, rows, m, d), F32),
                        pltpu.SemaphoreType.DMA((2, 2, X_HEADS))],
        compiler_params=_params(1),
        name="attn_cache",
    )(q, cache_k, cache_v)


def _route(s, sel):
    neg = -jnp.inf

    def top2(vals):
        m1 = functools.reduce(jnp.maximum, vals)
        i1 = jnp.full(m1.shape, len(vals) - 1, jnp.int32)
        for j in range(len(vals) - 2, -1, -1):
            i1 = jnp.where(vals[j] == m1, j, i1)
        rest = [jnp.where(i1 == j, neg, vals[j]) for j in range(len(vals))]
        m2 = functools.reduce(jnp.maximum, rest)
        i2 = jnp.full(m1.shape, len(vals) - 1, jnp.int32)
        for j in range(len(vals) - 2, -1, -1):
            i2 = jnp.where(rest[j] == m2, j, i2)
        return m1, m2, i1, i2

    groups = [top2(sel[g * EXPERTS_PER_GROUP:(g + 1) * EXPERTS_PER_GROUP]) for g in range(N_GROUPS)]
    score = [m1 + m2 for (m1, m2, _, _) in groups]
    best = functools.reduce(jnp.maximum, score)
    g_idx = jnp.full(best.shape, N_GROUPS - 1, jnp.int32)
    for g in range(N_GROUPS - 2, -1, -1):
        g_idx = jnp.where(score[g] == best, g, g_idx)
    e1 = jnp.zeros(best.shape, jnp.int32)
    e2 = jnp.zeros(best.shape, jnp.int32)
    for g in range(N_GROUPS):
        e1 = jnp.where(g_idx == g, g * EXPERTS_PER_GROUP + groups[g][2], e1)
        e2 = jnp.where(g_idx == g, g * EXPERTS_PER_GROUP + groups[g][3], e2)
    w1 = functools.reduce(jnp.add, [jnp.where(e1 == e, s[e], 0.0) for e in range(N_EXPERTS)])
    w2 = functools.reduce(jnp.add, [jnp.where(e2 == e, s[e], 0.0) for e in range(N_EXPERTS)])
    tot = w1 + w2
    return e1, e2, w1 / tot, w2 / tot


def _post_kernel(x1_ref, o_ref, wo_ref, lng_ref, lnb_ref, rwt_ref, rb_ref, tri_ref,
                 x2_ref, x2b_ref, rows_ref, cols_ref, seg_ref, *, tile):
    y = jnp.dot(o_ref[...], wo_ref[...], preferred_element_type=F32)
    x2 = _layer_norm(DN_ALPHA * x1_ref[...] + y, lng_ref[...], lnb_ref[...])
    x2_ref[...] = x2
    x2b_ref[...] = x2.astype(BF16)
    logits = lax.dot_general(rwt_ref[...], x2, (((1,), (1,)), ((), ())), preferred_element_type=F32,
                             precision=lax.Precision.HIGHEST)
    s_all = _sigmoid(logits)
    sel_all = s_all + rb_ref[...]
    s = [s_all[e:e + 1] for e in range(N_EXPERTS)]
    sel = [sel_all[e:e + 1] for e in range(N_EXPERTS)]
    e1, e2, w1, w2 = _route(s, sel)

    onehot = jnp.concatenate([jnp.where((e1 == e) | (e2 == e), 1.0, 0.0) for e in range(N_EXPERTS)], axis=0)
    prefix = jnp.dot(onehot.astype(BF16), tri_ref[...], preferred_element_type=F32)
    cnt = jnp.broadcast_to(jnp.sum(onehot, axis=-1, keepdims=True), (N_EXPERTS, LANES))
    cpad = jnp.floor((cnt + (SEG_ALIGN - 1.0)) * (1.0 / SEG_ALIGN)) * SEG_ALIGN
    ridx = lax.broadcasted_iota(jnp.int32, (N_EXPERTS, LANES), 0)
    inc = cpad
    k = 1
    while k < N_EXPERTS:
        inc = inc + jnp.where(ridx >= k, _shift_rows(inc, k), 0.0)
        k *= 2
    start = inc - cpad

    def pick(e_row, table):
        return functools.reduce(jnp.add, [jnp.where(e_row == e, table[e:e + 1], 0.0) for e in range(N_EXPERTS)])

    start_col = start[:, 0:1]
    pos1 = pick(e1, start_col) + pick(e1, prefix)
    pos2 = pick(e2, start_col) + pick(e2, prefix)
    rows = jnp.concatenate([pos1, pos2, w1, w2, jnp.zeros((SUBLANES - 4, tile), F32)], axis=0)
    rows_ref[...] = rows
    padded = jnp.concatenate([rows, jnp.zeros((LANES - SUBLANES, tile), F32)], axis=0)
    for j in range(tile // LANES):
        cols_ref[j * LANES:(j + 1) * LANES, :] = padded[:, j * LANES:(j + 1) * LANES].T
    seg_ref[0] = start
    seg_ref[1] = cnt


def _post(x1, o, p, router_wt, router_b, tri, tile):
    n, d = x1.shape
    wts, layer = p["wts"], p["layer"]
    small = [p["ln_g"][1].reshape(1, d), p["ln_b"][1].reshape(1, d), router_wt, router_b.reshape(N_EXPERTS, 1), tri]
    ins = [x1, o, wts["x_wo"]] + small
    in_specs = [pl.BlockSpec((tile, d), lambda i: (i, 0))] * 2 + [_layer_spec(wts["x_wo"], layer)]
    in_specs += [_resident(a.shape) for a in small]
    return pl.pallas_call(
        functools.partial(_post_kernel, tile=tile),
        grid=(n // tile,),
        in_specs=in_specs,
        out_specs=[pl.BlockSpec((tile, d), lambda i: (i, 0))] * 2 + [
            pl.BlockSpec((SUBLANES, tile), lambda i: (0, i)),
            pl.BlockSpec((tile, LANES), lambda i: (i, 0)),
            pl.BlockSpec((None, 2, N_EXPERTS, LANES), lambda i: (i, 0, 0, 0))],
        out_shape=[jax.ShapeDtypeStruct((n, d), F32), jax.ShapeDtypeStruct((n, d), BF16),
                   jax.ShapeDtypeStruct((SUBLANES, n), F32), jax.ShapeDtypeStruct((n, LANES), F32),
                   jax.ShapeDtypeStruct((n // tile, 2, N_EXPERTS, LANES), F32)],
        compiler_params=_params(1),
        name="post_attn_router",
    )(*ins)


def _moe_kernel(start_ref, cnt_ref, x2_ref, xb_ref, rows_ref, cols_ref, wg_ref, wu_ref, wd_ref, lng_ref, lnb_ref,
                out_ref, xs_ref, ys_ref, ws_ref, *, tile, rcap, rblk, tblk):
    i = pl.program_id(0)
    step = pl.program_id(1)

    @pl.when(step == 0)
    def _sort_rows():
        pos1, pos2 = rows_ref[0:1, :], rows_ref[1:2, :]
        w1, w2 = rows_ref[2:3, :], rows_ref[3:4, :]
        xb = xb_ref[...]
        for c in range(rcap // rblk):
            r = (c * rblk + lax.broadcasted_iota(jnp.int32, (rblk, tile), 0)).astype(F32)
            m1 = pos1 == r
            m2 = pos2 == r
            sel = jnp.where(m1 | m2, 1.0, 0.0).astype(BF16)
            xs_ref[c * rblk:(c + 1) * rblk, :] = jnp.dot(sel, xb, preferred_element_type=F32).astype(BF16)
            ws_ref[c * rblk:(c + 1) * rblk, :] = jnp.sum(jnp.where(m1, w1, 0.0) + jnp.where(m2, w2, 0.0),
                                                       axis=-1, keepdims=True)
        ys_ref[...] = jnp.zeros_like(ys_ref)

    for k in range(MOE_EXPERTS_PER_STEP):
        e = step * MOE_EXPERTS_PER_STEP + k
        start = start_ref[i * N_EXPERTS + e]
        n_chunks = cnt_ref[i * N_EXPERTS + e]

        def chunk_body(j, carry, k=k, start=start):
            r = pl.multiple_of(start + j * MOE_CHUNK, SEG_ALIGN)
            xs = xs_ref[pl.ds(r, MOE_CHUNK), :]
            gate = jnp.dot(xs, wg_ref[k], preferred_element_type=F32)
            up = jnp.dot(xs, wu_ref[k], preferred_element_type=F32)
            hid = gate * _sigmoid(gate) * up
            y = _bdot(hid, wd_ref[k])
            ys_ref[pl.ds(r, MOE_CHUNK), :] = (ws_ref[pl.ds(r, MOE_CHUNK), :] * y).astype(BF16)
            return carry

        lax.fori_loop(0, n_chunks, chunk_body, 0)

    @pl.when(step == pl.num_programs(1) - 1)
    def _finish():
        for c in range(tile // tblk):
            sl = slice(c * tblk, (c + 1) * tblk)
            r = lax.broadcasted_iota(jnp.int32, (tblk, rcap), 1).astype(F32)
            back = jnp.where((cols_ref[sl, 0:1] == r) | (cols_ref[sl, 1:2] == r), 1.0, 0.0).astype(BF16)
            y = jnp.dot(back, ys_ref[...], preferred_element_type=F32)
            out_ref[sl, :] = _layer_norm(DN_ALPHA * x2_ref[sl, :] + y, lng_ref[...], lnb_ref[...])


def _moe(x2, x2b, rows, cols, seg, p, tile):
    n, d = x2.shape
    wts, layer = p["wts"], p["layer"]
    de = wts["e_wg"].shape[-1]
    tblk = min(256, tile)
    rcap = -(-(2 * tile + N_EXPERTS * SEG_ALIGN + MOE_CHUNK) // SEG_ALIGN) * SEG_ALIGN
    rblk = max(b for b in range(SEG_ALIGN, 257, SEG_ALIGN) if rcap % b == 0)
    seg_start = seg[:, 0, :, 0].astype(jnp.int32).reshape(-1)
    seg_cnt = (seg[:, 1, :, 0].astype(jnp.int32).reshape(-1) + (MOE_CHUNK - 1)) // MOE_CHUNK
    row_spec = lambda w: pl.BlockSpec((tile, w), lambda i, e, s, c: (i, 0))
    grid_spec = pltpu.PrefetchScalarGridSpec(
        num_scalar_prefetch=2,
        grid=(n // tile, N_EXPERTS // MOE_EXPERTS_PER_STEP),
        in_specs=[row_spec(d), row_spec(d),
                  pl.BlockSpec((SUBLANES, tile), lambda i, e, s, c: (0, i)),
                  row_spec(LANES),
                  pl.BlockSpec((None, MOE_EXPERTS_PER_STEP, d, de), lambda i, e, s, c: (layer, e, 0, 0)),
                  pl.BlockSpec((None, MOE_EXPERTS_PER_STEP, d, de), lambda i, e, s, c: (layer, e, 0, 0)),
                  pl.BlockSpec((None, MOE_EXPERTS_PER_STEP, de, d), lambda i, e, s, c: (layer, e, 0, 0)),
                  pl.BlockSpec((1, d), lambda i, e, s, c: (0, 0)),
                  pl.BlockSpec((1, d), lambda i, e, s, c: (0, 0))],
        out_specs=row_spec(d),
        scratch_shapes=[pltpu.VMEM((rcap, d), BF16), pltpu.VMEM((rcap, d), BF16), pltpu.VMEM((rcap, 1), F32)],
    )
    return pl.pallas_call(
        functools.partial(_moe_kernel, tile=tile, rcap=rcap, rblk=rblk, tblk=tblk),
        grid_spec=grid_spec,
        out_shape=jax.ShapeDtypeStruct((n, d), F32),
        compiler_params=_params(2),
        name="moe",
    )(seg_start, seg_cnt, x2, x2b, rows, cols, wts["e_wg"], wts["e_wu"], wts["e_wd"],
      p["ln_g"][2].reshape(1, d), p["ln_b"][2].reshape(1, d))


def _pick(n, pref):
    return pref if n % pref == 0 else n


def kernel(x_prompt, x_sample, mem_prompt, cache_mem_k, cache_mem_v, state_conv, state_rglru, state_pool, state_hgrn, ln_in_g, ln_in_b, w_in, b_in, sgu_ln_g, sgu_ln_b, sgu_w, sgu_b, conv_w, conv_b, lru_wr, lru_br, lru_wi, lru_bi, lru_lambda, pool_w, pool_b, pool_scale, hgrn_lb, hgrn_norm_g, w_branch, w_out, x_wq, x_wk, x_wv, x_wo, ln_g, ln_b, router_w, router_b, e_wg, e_wu, e_wd):
    bp, tp, d = x_prompt.shape
    bs, ts, _ = x_sample.shape
    depth = w_in.shape[0]
    br = d // 2
    n_mem = mem_prompt.shape[1]
    hd = br // HG_HEADS
    mixw = 9 * br
    assert depth == MODEL_DEPTH

    lb_cum = jnp.cumsum(jax.nn.softmax(hgrn_lb.astype(F32), axis=0), axis=0)
    lbs = lb_cum - lb_cum[0]

    wts = dict(w_in=w_in, w_branch=w_branch, w_out=w_out, x_wq=x_wq, x_wo=x_wo, lru_wr=lru_wr, lru_wi=lru_wi,
               pool_w=pool_w, e_wg=e_wg, e_wu=e_wu, e_wd=e_wd)
    wts = {k: v.astype(BF16) for k, v in wts.items()}
    router_wt = router_w.T

    tile_p = SGU_CHUNK if tp % SGU_CHUNK == 0 else tp
    tile_s = SGU_CHUNK if ts % SGU_CHUNK == 0 else ts
    sub_p = _pick(tile_p, SUBLANES)
    sub_s = _pick(tile_s, SUBLANES)

    n_p, n_s = bp * tp, bs * ts
    xp = _input_ln(x_prompt.reshape(n_p, d), ln_in_g, ln_in_b, _pick(n_p, 512))
    xs = _input_ln(x_sample.reshape(n_s, d), ln_in_g, ln_in_b, _pick(n_s, 512))

    mem_flat = mem_prompt.reshape(bp * n_mem, d)
    mk_all, mv_all, mkb, mvb = _memory_kv(mem_flat, x_wk.astype(BF16), x_wv.astype(BF16), _pick(bp * n_mem, 512))
    mk_all = mk_all.reshape(depth, bp, n_mem, X_HEADS, d // X_HEADS)
    mv_all = mv_all.reshape(depth, bp, n_mem, X_HEADS, d // X_HEADS)
    mkb = mkb.reshape(depth, bp, n_mem, d)
    mvb = mvb.reshape(depth, bp, n_mem, d)

    moe_tile_p = _pick(n_p, MOE_TILE)
    moe_tile_s = _pick(n_s, MOE_TILE)

    def strict_upper(t):
        idx = jnp.arange(t)
        return (idx[:, None] < idx[None, :]).astype(BF16)

    tri_p, tri_s = strict_upper(moe_tile_p), strict_upper(moe_tile_s)

    outs_p, outs_s = [], []
    for l in range(depth):
        p = dict(wts=wts, layer=l, b_in=b_in[l], sgu_ln_g=sgu_ln_g[l],
                 sgu_ln_b=sgu_ln_b[l], sgu_w=sgu_w[l], sgu_b=sgu_b[l], conv_w=conv_w[l], conv_b=conv_b[l],
                 lru_br=lru_br[l], lru_bi=lru_bi[l], lru_lambda=lru_lambda[l], pool_b=pool_b[l],
                 pool_scale=pool_scale[l], hgrn_norm_g=hgrn_norm_g[l], ln_g=ln_g[l], ln_b=ln_b[l])

        obr_p, conv_p, h_p, pool_p, s_p = _seqmix(xp, p, lbs[l], None, batch=bp, seq=tp, tile=tile_p, sub=sub_p,
                                                   pos0=0, emit_v=False, projected=False)
        zs = _project(xs, p, _pick(n_s, 256))
        obr_s, conv_s, h_s, pool_s, s_s, v_s = _seqmix(
            zs, p, lbs[l], (state_conv[l], state_rglru[l], state_pool[l], state_hgrn[l]),
            batch=bs, seq=ts, tile=tile_s, sub=sub_s, pos0=PAST_LEN, emit_v=True, projected=True)
        outs_p.append((conv_p, h_p.reshape(bp, br), pool_p, s_p))
        outs_s.append((conv_s, h_s.reshape(bs, br), pool_s, s_s, v_s.reshape(bs, ts, br)))

        x1p, qp = _merge(xp, obr_p, p, _pick(n_p, 512))
        x1s, qs = _merge(xs, obr_s, p, _pick(n_s, 512))
        op = _attn(qp, mkb, mvb, l, batch=bp, seq=tp, tile=_pick(tp, 512))
        os_ = _attn_cache(qs, cache_mem_k, cache_mem_v, l, batch=bs, seq=ts, rows=_pick(bs, CACHE_ROWS))
        x2p, x2pb, rows_p, cols_p, seg_p = _post(x1p, op, p, router_wt, router_b, tri_p, moe_tile_p)
        x2s, x2sb, rows_s, cols_s, seg_s = _post(x1s, os_, p, router_wt, router_b, tri_s, moe_tile_s)
        xp = _moe(x2p, x2pb, rows_p, cols_p, seg_p, p, moe_tile_p)
        xs = _moe(x2s, x2sb, rows_s, cols_s, seg_s, p, moe_tile_s)

    stack = lambda outs, i: jnp.stack([o[i] for o in outs])
    return (xp.reshape(bp, tp, d), xs.reshape(bs, ts, d), mk_all, mv_all,
            stack(outs_p, 0), stack(outs_p, 1), stack(outs_p, 2), stack(outs_p, 3),
            stack(outs_s, 0), stack(outs_s, 1), stack(outs_s, 2), stack(outs_s, 3), stack(outs_s, 4))
```
